```python
import math
import jax
import jax.numpy as jnp
from jax import lax
import numpy as np

D_MODEL = 1024
BATCH = 8
SEQ = 4096
DEPTH = 2

HEAD_DIM = 64
N_MIXERS = 4
GROUP_HEADS = D_MODEL // HEAD_DIM // N_MIXERS
GROUP_WIDTH = GROUP_HEADS * HEAD_DIM
MIX_WIDTH = N_MIXERS * GROUP_WIDTH

MLA_Q_RANK = 256
MLA_KV_RANK = 128
MLA_NOPE = 64
MLA_ROPE = 32
MLA_V = HEAD_DIM

DIL_CONFIGS = ((128, 1), (512, 4), (2048, 16))

DIFF_QK = HEAD_DIM // 2
DIFF_V = HEAD_DIM

D_FF_DENSE = 2816
N_EXPERTS = 8
TOP_K = 2
D_FF_EXPERT = 3584
N_DENSE = (DEPTH + 1) // 2
N_MOE = DEPTH // 2

ROPE_THETA = 10000.0
NORM_EPS = 1e-6
BLOCK = 128
NEG_INF = -1e30

IN_SIZES = (MLA_Q_RANK, MLA_KV_RANK, MLA_ROPE,
            GROUP_WIDTH, GROUP_WIDTH, GROUP_WIDTH,
            GROUP_WIDTH, GROUP_WIDTH, GROUP_WIDTH, GROUP_HEADS,
            GROUP_WIDTH, GROUP_WIDTH, GROUP_WIDTH)
IN_WIDTH = 2724

kernel_name = 'hybrid_parallel_group_decoder'


def _rmsnorm(x, g):
    xf = x.astype(jnp.float32)
    y = xf * lax.rsqrt(jnp.mean(xf * xf, axis=-1, keepdims=True) + NORM_EPS)
    return (y * g.astype(jnp.float32)).astype(x.dtype)


def _rope(x, pos):
    half = x.shape[-1] // 2
    inv = ROPE_THETA ** (-jnp.arange(half, dtype=jnp.float32) / half)
    ang = pos.astype(jnp.float32)[:, None] * inv[None, :]
    cos = jnp.cos(ang)[None, :, None, :]
    sin = jnp.sin(ang)[None, :, None, :]
    xf = x.astype(jnp.float32)
    x1, x2 = xf[..., :half], xf[..., half:]
    return jnp.concatenate([x1 * cos - x2 * sin, x2 * cos + x1 * sin], axis=-1).astype(x.dtype)


def _q_block(t, i):
    return lax.dynamic_slice_in_dim(t, i * BLOCK, BLOCK, axis=1)


def _causal_mask(i, s_len):
    q_pos = i * BLOCK + jnp.arange(BLOCK)
    return jnp.arange(s_len)[None, :] <= q_pos[:, None]


def _masked_softmax(s, mask):
    return jax.nn.softmax(jnp.where(mask, s, NEG_INF), axis=-1)


def _causal_sweep(block_fn, s_len):
    out = lax.map(block_fn, jnp.arange(s_len // BLOCK))
    nblk, b, blk, h, dv = out.shape
    return out.transpose(1, 0, 2, 3, 4).reshape(b, nblk * blk, h, dv)


def _mla(cq, ckv, kr, g_ql, g_kvl, w_uq, w_ukv, pos):
    b, s_len, _ = cq.shape
    q = (_rmsnorm(cq, g_ql) @ w_uq).reshape(b, s_len, GROUP_HEADS, MLA_NOPE + MLA_ROPE)
    q_nope, q_rot = q[..., :MLA_NOPE], _rope(q[..., MLA_NOPE:], pos)
    kv = (_rmsnorm(ckv, g_kvl) @ w_ukv).reshape(b, s_len, GROUP_HEADS, MLA_NOPE + MLA_V)
    k_nope, v = kv[..., :MLA_NOPE], kv[..., MLA_NOPE:]
    k_rot = _rope(kr[:, :, None, :], pos)[:, :, 0]
    scale = (MLA_NOPE + MLA_ROPE) ** -0.5

    def block_fn(i):
        s = (jnp.einsum('bqhd,bkhd->bhqk', _q_block(q_nope, i), k_nope)
             + jnp.einsum('bqhd,bkd->bhqk', _q_block(q_rot, i), k_rot)).astype(jnp.float32) * scale
        p = _masked_softmax(s, _causal_mask(i, s_len))
        return jnp.einsum('bhqk,bkhd->bqhd', p.astype(v.dtype), v)

    return _causal_sweep(block_fn, s_len)


def _dilated_branch(q, k, v, window, dilation):
    b, s_len, h, dh = q.shape
    n = window // dilation
    span = n * dilation
    s_pad = -(-s_len // span) * span
    nb = s_pad // span

    def to_sub(t):
        t = jnp.pad(t, ((0, 0), (0, s_pad - s_len), (0, 0), (0, 0)))
        t = t.reshape(b, s_pad // dilation, dilation, h, dh).transpose(0, 2, 3, 1, 4)
        return t.reshape(b, dilation, h, nb, n, dh)

    def with_prev(t):
        prev = jnp.concatenate([jnp.zeros_like(t[:, :, :, :1]), t[:, :, :, :-1]], axis=3)
        return jnp.concatenate([prev, t], axis=4)

    qs = to_sub(q)
    kk, vv = with_prev(to_sub(k)), with_prev(to_sub(v))
    s = jnp.einsum('brhnid,brhnjd->brhnij', qs, kk).astype(jnp.float32) * dh ** -0.5
    i_idx = jnp.arange(n)[:, None]
    c_idx = jnp.arange(2 * n)[None, :]
    band = (c_idx >= i_idx) & (c_idx <= i_idx + n)
    valid = band[None] & ((jnp.arange(nb) > 0)[:, None, None] | (c_idx >= n)[None])
    s = jnp.where(valid, s, NEG_INF)
    m = jnp.max(s, axis=-1, keepdims=True)
    p = jnp.exp(s - m)
    l = jnp.sum(p, axis=-1, keepdims=True)
    o = jnp.einsum('brhnij,brhnjd->brhnid', p.astype(v.dtype), vv).astype(jnp.float32) / l
    lse = (m + jnp.log(l))[..., 0]
    o = o.reshape(b, dilation, h, s_pad // dilation, dh).transpose(0, 3, 1, 2, 4).reshape(b, s_pad, h, dh)[:, :s_len]
    lse = lse.reshape(b, dilation, h, s_pad // dilation).transpose(0, 3, 1, 2).reshape(b, s_pad, h)[:, :s_len]
    return o, lse


def _dilated_attention(q, k, v):
    outs, lses = [], []
    for window, dilation in DIL_CONFIGS:
        o, lse = _dilated_branch(q, k, v, window, dilation)
        outs.append(o)
        lses.append(lse)
    w = jax.nn.softmax(jnp.stack(lses, axis=0), axis=0)
    return jnp.sum(w[..., None] * jnp.stack(outs, axis=0), axis=0)


def _forgetting_attention(q, k, v, f_logit):
    s_len = q.shape[1]
    log_f = jax.nn.log_sigmoid(f_logit.astype(jnp.float32))
    c = jnp.cumsum(log_f, axis=1).transpose(0, 2, 1)
    scale = HEAD_DIM ** -0.5

    def block_fn(i):
        c_q = lax.dynamic_slice_in_dim(c, i * BLOCK, BLOCK, axis=2)
        s = jnp.einsum('bqhd,bkhd->bhqk', _q_block(q, i), k).astype(jnp.float32) * scale
        s = s + c_q[..., :, None] - c[..., None, :]
        p = _masked_softmax(s, _causal_mask(i, s_len))
        return jnp.einsum('bhqk,bkhd->bqhd', p.astype(v.dtype), v)

    return _causal_sweep(block_fn, s_len)


def _differential_attention(q, k, v, lam, g_sub, lam_init, pos):
    b, s_len, _ = q.shape
    q = _rope(q.reshape(b, s_len, 2 * GROUP_HEADS, DIFF_QK), pos).reshape(b, s_len, GROUP_HEADS, 2, DIFF_QK)
    k = _rope(k.reshape(b, s_len, 2 * GROUP_HEADS, DIFF_QK), pos).reshape(b, s_len, GROUP_HEADS, 2, DIFF_QK)
    v = v.reshape(b, s_len, GROUP_HEADS, DIFF_V)
    q1, q2 = q[..., 0, :], q[..., 1, :]
    k1, k2 = k[..., 0, :], k[..., 1, :]
    scale = DIFF_QK ** -0.5

    def block_fn(i):
        mask = _causal_mask(i, s_len)
        s1 = jnp.einsum('bqhd,bkhd->bhqk', _q_block(q1, i), k1).astype(jnp.float32) * scale
        s2 = jnp.einsum('bqhd,bkhd->bhqk', _q_block(q2, i), k2).astype(jnp.float32) * scale
        p = _masked_softmax(s1, mask) - lam * _masked_softmax(s2, mask)
        return jnp.einsum('bhqk,bkhd->bqhd', p.astype(v.dtype), v)

    o = _causal_sweep(block_fn, s_len)
    return _rmsnorm(o, g_sub) * (1.0 - lam_init)


def _swiglu(h, w_gate, w_up, w_down):
    return (jax.nn.silu(h @ w_gate) * (h @ w_up)) @ w_down


def _moe(h, w_router, w_gate, w_up, w_down):
    b, s_len, d = h.shape
    t = h.reshape(b * s_len, d)
    logits = (t @ w_router).astype(jnp.float32)
    top_val, top_idx = lax.top_k(logits, TOP_K)
    weights = jax.nn.softmax(top_val, axis=-1)
    gates = jnp.sum(jax.nn.one_hot(top_idx, N_EXPERTS, dtype=jnp.float32) * weights[..., None], axis=1)
    y = jnp.zeros((b * s_len, d), jnp.float32)
    for e in range(N_EXPERTS):
        y = y + gates[:, e:e + 1] * _swiglu(t, w_gate[e], w_up[e], w_down[e]).astype(jnp.float32)
    return y.reshape(b, s_len, d).astype(h.dtype)


def _normal(key, shape, scale):
    return jax.random.normal(key, shape, jnp.float32) * scale


def setup_inputs(seed: int = 0) -> dict:
    key = jax.random.key(seed)
    ks = jax.random.split(key, 24)
    gain = lambda k, shape: 1.0 + _normal(k, shape, 0.01)
    return {
        'x': _normal(ks[0], (BATCH, SEQ, D_MODEL), 1.0),
        'g_mix': gain(ks[1], (DEPTH, D_MODEL)),
        'w_in': _normal(ks[2], (DEPTH, D_MODEL, IN_WIDTH), D_MODEL ** -0.5),
        'b_forget': _normal(ks[3], (DEPTH, GROUP_HEADS), 0.1),
        'g_q_lat': gain(ks[4], (DEPTH, MLA_Q_RANK)),
        'g_kv_lat': gain(ks[5], (DEPTH, MLA_KV_RANK)),
        'w_uq': _normal(ks[6], (DEPTH, MLA_Q_RANK, GROUP_HEADS * (MLA_NOPE + MLA_ROPE)), MLA_Q_RANK ** -0.5),
        'w_ukv': _normal(ks[7], (DEPTH, MLA_KV_RANK, GROUP_HEADS * (MLA_NOPE + MLA_V)), MLA_KV_RANK ** -0.5),
        'lambda_q1': _normal(ks[8], (DEPTH, DIFF_QK), 0.1),
        'lambda_k1': _normal(ks[9], (DEPTH, DIFF_QK), 0.1),
        'lambda_q2': _normal(ks[10], (DEPTH, DIFF_QK), 0.1),
        'lambda_k2': _normal(ks[11], (DEPTH, DIFF_QK), 0.1),
        'g_diff_sub': gain(ks[12], (DEPTH, DIFF_V)),
        'w_out': _normal(ks[13], (DEPTH, MIX_WIDTH, D_MODEL), MIX_WIDTH ** -0.5),
        'g_ffn': gain(ks[14], (DEPTH, D_MODEL)),
        'w_ffn_gate': _normal(ks[15], (N_DENSE, D_MODEL, D_FF_DENSE), D_MODEL ** -0.5),
        'w_ffn_up': _normal(ks[16], (N_DENSE, D_MODEL, D_FF_DENSE), D_MODEL ** -0.5),
        'w_ffn_down': _normal(ks[17], (N_DENSE, D_FF_DENSE, D_MODEL), D_FF_DENSE ** -0.5),
        'w_router': _normal(ks[18], (N_MOE, D_MODEL, N_EXPERTS), D_MODEL ** -0.5),
        'w_exp_gate': _normal(ks[19], (N_MOE, N_EXPERTS, D_MODEL, D_FF_EXPERT), D_MODEL ** -0.5),
        'w_exp_up': _normal(ks[20], (N_MOE, N_EXPERTS, D_MODEL, D_FF_EXPERT), D_MODEL ** -0.5),
        'w_exp_down': _normal(ks[21], (N_MOE, N_EXPERTS, D_FF_EXPERT, D_MODEL), D_FF_EXPERT ** -0.5),
        'g_final': gain(ks[22], (D_MODEL,)),
    }


def reference(x, g_mix, w_in, b_forget, g_q_lat, g_kv_lat, w_uq, w_ukv,
              lambda_q1, lambda_k1, lambda_q2, lambda_k2, g_diff_sub, w_out,
              g_ffn, w_ffn_gate, w_ffn_up, w_ffn_down,
              w_router, w_exp_gate, w_exp_up, w_exp_down, g_final):
    b, s_len, _ = x.shape
    pos = jnp.arange(s_len)
    offsets = []
    acc = 0
    for size in IN_SIZES[:-1]:
        acc += size
        offsets.append(acc)
    for l in range(DEPTH):
        h = _rmsnorm(x, g_mix[l])
        z = h @ w_in[l]
        (cq, ckv, kr, q_b, k_b, v_b, q_c, k_c, v_c, f_c, q_d, k_d, v_d) = jnp.split(z, offsets, axis=-1)
        heads = lambda t: t.reshape(b, s_len, GROUP_HEADS, HEAD_DIM)

        y_a = _mla(cq, ckv, kr, g_q_lat[l], g_kv_lat[l], w_uq[l], w_ukv[l], pos)
        y_b = _dilated_attention(_rope(heads(q_b), pos), _rope(heads(k_b), pos), heads(v_b))
        y_c = _forgetting_attention(heads(q_c), heads(k_c), heads(v_c), f_c + b_forget[l])
        lam_init = 0.8 - 0.6 * math.exp(-0.3 * l)
        lam = (jnp.exp(jnp.sum(lambda_q1[l] * lambda_k1[l]).astype(jnp.float32))
               - jnp.exp(jnp.sum(lambda_q2[l] * lambda_k2[l]).astype(jnp.float32)) + lam_init)
        y_d = _differential_attention(q_d, k_d, v_d, lam, g_diff_sub[l], lam_init, pos)

        y = jnp.concatenate([t.reshape(b, s_len, GROUP_WIDTH).astype(x.dtype) for t in (y_a, y_b, y_c, y_d)], axis=-1)
        x = x + y @ w_out[l]

        h = _rmsnorm(x, g_ffn[l])
        if l % 2 == 0:
            x = x + _swiglu(h, w_ffn_gate[l // 2], w_ffn_up[l // 2], w_ffn_down[l // 2])
        else:
            x = x + _moe(h, w_router[l // 2], w_exp_gate[l // 2], w_exp_up[l // 2], w_exp_down[l // 2])
    return _rmsnorm(x, g_final)
```

```python
import functools
import math

import jax
import jax.numpy as jnp
from jax import lax
from jax.experimental import pallas as pl
from jax.experimental.pallas import tpu as pltpu

F32 = jnp.float32
BF16 = jnp.bfloat16

D_MODEL = 1024
HEAD_DIM = 64
GROUP_HEADS = 4
GROUP_WIDTH = GROUP_HEADS * HEAD_DIM
MLA_Q_RANK = 256
MLA_KV_RANK = 128
MLA_NOPE = 64
MLA_ROPE = 32
DIL_CONFIGS = ((128, 1), (512, 4), (2048, 16))
DIL_MAX_WINDOW = max(w for w, _ in DIL_CONFIGS)
DIFF_QK = HEAD_DIM // 2
N_EXPERTS = 8
ROPE_THETA = 10000.0
NORM_EPS = 1e-6
NEG_INF = -1e30
LANES = 128
IN_SIZES = (MLA_Q_RANK, MLA_KV_RANK, MLA_ROPE,
            GROUP_WIDTH, GROUP_WIDTH, GROUP_WIDTH,
            GROUP_WIDTH, GROUP_WIDTH, GROUP_WIDTH, GROUP_HEADS,
            GROUP_WIDTH, GROUP_WIDTH, GROUP_WIDTH)

WP_CQ, WP_CKV, WP_KR4, WP_F = 0, 256, 384, 896
WP_QB = 1024
WP_WIDTH = WP_QB + 9 * GROUP_WIDTH

ZZ_QA, ZZ_KA, ZZ_VA = 0, 512, 1024
ZZ_QB = 1280
ZZ_WIDTH = ZZ_QB + 9 * GROUP_WIDTH

ROPE_A, ROPE_B, ROPE_D = 0, 1, 2

VMEM_LIMIT = 56 * 1024 * 1024


def _params(*sem):
    return pltpu.CompilerParams(dimension_semantics=sem, vmem_limit_bytes=VMEM_LIMIT)


def _rope_tables(s_len):
    pos = jnp.arange(s_len, dtype=F32)[:, None]
    lane = jnp.arange(LANES)[None, :]

    def kind(rel, width, active):
        half = width // 2
        idx = (rel % half).astype(F32)
        inv = ROPE_THETA ** (-idx / half)
        ang = pos * inv
        cos, sin = jnp.cos(ang), jnp.sin(ang)
        first = active & (rel < half)
        second = active & (rel >= half)
        c = jnp.where(active, cos, 1.0)
        sm = jnp.where(first, -sin, 0.0)
        sp = jnp.where(second, sin, 0.0)
        return [c, sm, sp]

    a_active = (lane >= MLA_NOPE) & (lane < MLA_NOPE + MLA_ROPE)
    tabs = (kind(jnp.where(a_active, lane - MLA_NOPE, 0), MLA_ROPE, a_active)
            + kind(lane % HEAD_DIM, HEAD_DIM, lane >= 0)
            + kind(lane % DIFF_QK, DIFF_QK, lane >= 0))
    return jnp.stack([jnp.broadcast_to(t, (s_len, LANES)).astype(F32) for t in tabs])


_ROPE_HALF = {ROPE_A: MLA_ROPE // 2, ROPE_B: HEAD_DIM // 2, ROPE_D: DIFF_QK // 2}


def _rope(blk, tab_ref, kind):
    half = _ROPE_HALF[kind]
    c = tab_ref[3 * kind]
    sm = tab_ref[3 * kind + 1]
    sp = tab_ref[3 * kind + 2]
    return blk * c + pltpu.roll(blk, LANES - half, 1) * sm + pltpu.roll(blk, half, 1) * sp


def _rms(x, g):
    return x * lax.rsqrt(jnp.mean(x * x, axis=-1, keepdims=True) + NORM_EPS) * g


def _proj_in_kernel(x_ref, g_ref, w_ref, gql_ref, gkvl_ref, wuq_ref, wuk_ref, wuv_ref, tab_ref,
                    zz_ref, zf_ref):
    h = _rms(x_ref[0], g_ref[...]).astype(BF16)

    def proj(c0, width):
        return jnp.dot(h, w_ref[:, c0:c0 + width], preferred_element_type=F32)

    def put(col, val):
        zz_ref[0, :, col:col + LANES] = val.astype(BF16)

    cqn = _rms(proj(WP_CQ, MLA_Q_RANK), gql_ref[...]).astype(BF16)
    qa = jnp.dot(cqn, wuq_ref[...], preferred_element_type=F32)
    ckvn = _rms(proj(WP_CKV, MLA_KV_RANK), gkvl_ref[...]).astype(BF16)
    ka = jnp.dot(ckvn, wuk_ref[...], preferred_element_type=F32) + proj(WP_KR4, GROUP_HEADS * LANES)
    scale_a = (MLA_NOPE + MLA_ROPE) ** -0.5
    for hh in range(GROUP_HEADS):
        sl = slice(hh * LANES, (hh + 1) * LANES)
        put(ZZ_QA + hh * LANES, _rope(qa[:, sl], tab_ref, ROPE_A) * scale_a)
        put(ZZ_KA + hh * LANES, _rope(ka[:, sl], tab_ref, ROPE_A))
    zz_ref[0, :, ZZ_VA:ZZ_VA + GROUP_WIDTH] = jnp.dot(
        ckvn, wuv_ref[...], preferred_element_type=F32).astype(BF16)

    zf_ref[0] = proj(WP_F, LANES)

    plan = ((ROPE_B, HEAD_DIM ** -0.5), (ROPE_B, 1.0), (None, 1.0),
            (None, HEAD_DIM ** -0.5), (None, 1.0), (None, 1.0),
            (ROPE_D, DIFF_QK ** -0.5), (ROPE_D, 1.0), (None, 1.0))
    for t, (kind, scale) in enumerate(plan):
        z = proj(WP_QB + t * GROUP_WIDTH, GROUP_WIDTH)
        for half in range(GROUP_WIDTH // LANES):
            blk = z[:, half * LANES:(half + 1) * LANES]
            if kind is not None:
                blk = _rope(blk, tab_ref, kind)
            if scale != 1.0:
                blk = blk * scale
            put(ZZ_QB + t * GROUP_WIDTH + half * LANES, blk)


def _proj_in(x, g, w, gql, gkvl, wuq, wuk, wuv, tabs, tm):
    b, s_len, d = x.shape
    const = lambda shape: pl.BlockSpec(shape, lambda bi, si: (0,) * len(shape))
    return pl.pallas_call(
        _proj_in_kernel,
        out_shape=(jax.ShapeDtypeStruct((b, s_len, ZZ_WIDTH), BF16),
                   jax.ShapeDtypeStruct((b, s_len, LANES), F32)),
        grid=(b, s_len // tm),
        in_specs=[
            pl.BlockSpec((1, tm, d), lambda bi, si: (bi, si, 0)),
            const((1, d)), const((d, WP_WIDTH)),
            const((1, MLA_Q_RANK)), const((1, MLA_KV_RANK)),
            const((MLA_Q_RANK, GROUP_HEADS * LANES)),
            const((MLA_KV_RANK, GROUP_HEADS * LANES)),
            const((MLA_KV_RANK, GROUP_WIDTH)),
            pl.BlockSpec((9, tm, LANES), lambda bi, si: (0, si, 0)),
        ],
        out_specs=(pl.BlockSpec((1, tm, ZZ_WIDTH), lambda bi, si: (bi, si, 0)),
                   pl.BlockSpec((1, tm, LANES), lambda bi, si: (bi, si, 0))),
        compiler_params=_params("parallel", "parallel"),
        name="proj_in",
    )(x, g, w, gql, gkvl, wuq, wuk, wuv, tabs)


def _forget_cumsum_kernel(zf_ref, bias_ref, ccol_ref, crow_ref, carry_ref):
    @pl.when(pl.program_id(1) == 0)
    def _():
        carry_ref[...] = jnp.zeros_like(carry_ref)

    t = zf_ref[0] + bias_ref[...]
    log_f = jnp.minimum(t, 0.0) - jnp.log(1.0 + jnp.exp(-jnp.abs(t)))
    tc = log_f.shape[0]
    row = lax.broadcasted_iota(jnp.int32, (tc, tc), 0)
    col = lax.broadcasted_iota(jnp.int32, (tc, tc), 1)
    tri = jnp.where(col <= row, 1.0, 0.0).astype(F32)
    cs = jnp.dot(tri, log_f, precision=lax.Precision.HIGHEST, preferred_element_type=F32) + carry_ref[...]
    ccol_ref[0] = cs
    carry_ref[...] = cs[tc - 1:tc, :]
    crow_ref[0] = cs.T[0:8, :]


def _forget_cumsum(zf, bias, tc):
    b, s_len, _ = zf.shape
    return pl.pallas_call(
        _forget_cumsum_kernel,
        out_shape=(jax.ShapeDtypeStruct((b, s_len, LANES), F32),
                   jax.ShapeDtypeStruct((b, 8, s_len), F32)),
        grid=(b, s_len // tc),
        in_specs=[pl.BlockSpec((1, tc, LANES), lambda bi, si: (bi, si, 0)),
                  pl.BlockSpec((1, LANES), lambda bi, si: (0, 0))],
        out_specs=(pl.BlockSpec((1, tc, LANES), lambda bi, si: (bi, si, 0)),
                   pl.BlockSpec((1, 8, tc), lambda bi, si: (bi, 0, si))),
        scratch_shapes=[pltpu.VMEM((1, LANES), F32)],
        compiler_params=_params("parallel", "arbitrary"),
        name="forget_cumsum",
    )(zf, bias)


def _dilated_log_multiplicity(t):
    n_tiles = DIL_MAX_WINDOW // t + 1
    k = jnp.arange(n_tiles)[:, None, None]
    i = jnp.arange(t)[None, :, None]
    j = jnp.arange(t)[None, None, :]
    delta = k * t + i - j
    mult = jnp.zeros(delta.shape, F32)
    for window, dilation in DIL_CONFIGS:
        mult = mult + ((delta >= 0) & (delta <= window) & (delta % dilation == 0)).astype(F32)
    return jnp.where(mult > 0, jnp.log(jnp.maximum(mult, 1.0)), NEG_INF).astype(F32)


def _attn_kernel(*refs, mixer, t, lam_init):
    q_ref, k_ref, v_ref = refs[:3]
    o_ref = refs[-1]
    extra = refs[3:-1]
    g = pl.program_id(1)
    qi = pl.program_id(2)
    sep = mixer == "mla"
    nsub = 4 if mixer == "diff" else 2
    nset = 2 if mixer == "diff" else 1
    sub_w = LANES // nsub

    lane = lax.broadcasted_iota(jnp.int32, (1, LANES), 1)
    lane_lo = lane < HEAD_DIM
    q = q_ref[0]
    if sep:
        qs = [q[:, hh * LANES:(hh + 1) * LANES] for hh in range(nsub)]
    else:
        qs = [jnp.where((lane >= hh * sub_w) & (lane < (hh + 1) * sub_w), q, jnp.zeros_like(q))
              for hh in range(nsub)]

    if mixer == "fox":
        ccol_ref, crow_ref = extra
        ccol = ccol_ref[0]
        cq = [jnp.sum(jnp.where(lane == 2 * g + hh, ccol, 0.0), axis=1, keepdims=True)
              for hh in range(nsub)]
    if mixer == "dil":
        (lm_ref,) = extra

    row = lax.broadcasted_iota(jnp.int32, (t, t), 0)
    col = lax.broadcasted_iota(jnp.int32, (t, t), 1)
    causal = col <= row

    def step(j, carry, diag):
        ms, ls, accs = carry
        off = pl.multiple_of(j * t, t)
        k = k_ref[0, pl.ds(off, t), :]
        v = v_ref[0, pl.ds(off, t), :]
        new_m, new_l, alphas, pvs = [], [], [], []
        for hh in range(nsub):
            kk = k[:, hh * LANES:(hh + 1) * LANES] if sep else k
            s = lax.dot_general(qs[hh], kk, (((1,), (1,)), ((), ())), preferred_element_type=F32)
            if mixer == "fox":
                s = s + (cq[hh] - crow_ref[0, pl.ds(2 * g + hh, 1), pl.ds(off, t)])
            if mixer == "dil":
                s = s + lm_ref[qi - j]
            elif diag:
                s = jnp.where(causal, s, NEG_INF)
            m_new = jnp.maximum(ms[hh], jnp.max(s, axis=1, keepdims=True))
            p = jnp.exp(s - m_new)
            alpha = jnp.exp(ms[hh] - m_new)
            new_l.append(alpha * ls[hh] + jnp.sum(p, axis=1, keepdims=True))
            new_m.append(m_new)
            alphas.append(alpha)
            pvs.append(jnp.dot(p.astype(BF16), v, preferred_element_type=F32))
        new_acc = []
        for st in range(nset):
            a0, a1 = (0, 1) if nsub == 2 else (st, 2 + st)
            new_acc.append(accs[st] * jnp.where(lane_lo, alphas[a0], alphas[a1])
                           + jnp.where(lane_lo, pvs[a0], pvs[a1]))
        return tuple(new_m), tuple(new_l), tuple(new_acc)

    init = (tuple(jnp.full((t, 1), NEG_INF, F32) for _ in range(nsub)),
            tuple(jnp.zeros((t, 1), F32) for _ in range(nsub)),
            tuple(jnp.zeros((t, LANES), F32) for _ in range(nset)))
    lo = jnp.maximum(qi - DIL_MAX_WINDOW // t, 0) if mixer == "dil" else 0
    carry = lax.fori_loop(lo, qi, lambda j, c: step(j, c, False), init)
    ms, ls, accs = step(qi, carry, True)

    if mixer != "diff":
        o_ref[0] = (accs[0] / jnp.where(lane_lo, ls[0], ls[1])).astype(o_ref.dtype)
    else:
        lamv_ref, gsub_ref = extra
        lamv = lamv_ref[...]
        e1 = jnp.exp(jnp.sum(lamv[0:1] * lamv[1:2], axis=1, keepdims=True))
        e2 = jnp.exp(jnp.sum(lamv[2:3] * lamv[3:4], axis=1, keepdims=True))
        lam = e1 - e2 + lam_init
        o = (accs[0] / jnp.where(lane_lo, ls[0], ls[2])
             - lam * (accs[1] / jnp.where(lane_lo, ls[1], ls[3])))
        sq = o * o
        ms_lo = jnp.sum(jnp.where(lane_lo, sq, 0.0), axis=1, keepdims=True) * (1.0 / HEAD_DIM)
        ms_hi = jnp.sum(jnp.where(lane_lo, 0.0, sq), axis=1, keepdims=True) * (1.0 / HEAD_DIM)
        rs = jnp.where(lane_lo, lax.rsqrt(ms_lo + NORM_EPS), lax.rsqrt(ms_hi + NORM_EPS))
        o_ref[0] = (o * rs * gsub_ref[...] * (1.0 - lam_init)).astype(o_ref.dtype)


def _attention(zz, mixer, t, q_col, k_col, v_col, extra=(), extra_specs=(), lam_init=0.0):
    b, s_len, _ = zz.shape
    qw = 2 * LANES if mixer == "mla" else LANES
    n_groups = GROUP_WIDTH // LANES
    qb, kb, vb = q_col // qw, k_col // qw, v_col // LANES
    return pl.pallas_call(
        functools.partial(_attn_kernel, mixer=mixer, t=t, lam_init=lam_init),
        out_shape=jax.ShapeDtypeStruct((b, s_len, GROUP_WIDTH), BF16),
        grid=(b, n_groups, s_len // t),
        in_specs=[pl.BlockSpec((1, t, qw), lambda bi, g, qi: (bi, qi, qb + g)),
                  pl.BlockSpec((1, s_len, qw), lambda bi, g, qi: (bi, 0, kb + g)),
                  pl.BlockSpec((1, s_len, LANES), lambda bi, g, qi: (bi, 0, vb + g)),
                  *extra_specs],
        out_specs=pl.BlockSpec((1, t, LANES), lambda bi, g, qi: (bi, qi, g)),
        compiler_params=_params("parallel", "parallel", "arbitrary"),
        name="attn_" + mixer,
    )(zz, zz, zz, *extra)


def _proj_out_kernel(x_ref, ya_ref, yb_ref, yc_ref, yd_ref, w_ref, o_ref):
    acc = x_ref[...]
    for i, y_ref in enumerate((ya_ref, yb_ref, yc_ref, yd_ref)):
        acc = acc + jnp.dot(y_ref[...], w_ref[i * GROUP_WIDTH:(i + 1) * GROUP_WIDTH, :],
                            preferred_element_type=F32)
    o_ref[...] = acc


def _proj_out(x2, ys, w, tm):
    n, d = x2.shape
    yspec = pl.BlockSpec((tm, GROUP_WIDTH), lambda i: (i, 0))
    return pl.pallas_call(
        _proj_out_kernel,
        out_shape=jax.ShapeDtypeStruct((n, d), F32),
        grid=(n // tm,),
        in_specs=[pl.BlockSpec((tm, d), lambda i: (i, 0)), yspec, yspec, yspec, yspec,
                  pl.BlockSpec((4 * GROUP_WIDTH, d), lambda i: (0, 0))],
        out_specs=pl.BlockSpec((tm, d), lambda i: (i, 0)),
        compiler_params=_params("parallel"),
        name="proj_out",
    )(x2, *ys, w)


def _silu(gate):
    return gate / (1.0 + jnp.exp(-gate))


def _ffn_kernel(x_ref, g_ref, wg_ref, wu_ref, wd_ref, o_ref, h_ref, acc_ref):
    f = pl.program_id(1)

    @pl.when(f == 0)
    def _():
        x = x_ref[...]
        h_ref[...] = _rms(x, g_ref[...]).astype(BF16)
        acc_ref[...] = x

    h = h_ref[...]
    gate = jnp.dot(h, wg_ref[...], preferred_element_type=F32)
    up = jnp.dot(h, wu_ref[...], preferred_element_type=F32)
    act = (_silu(gate) * up).astype(BF16)
    acc_ref[...] += jnp.dot(act, wd_ref[...], preferred_element_type=F32)

    @pl.when(f == pl.num_programs(1) - 1)
    def _():
        o_ref[...] = acc_ref[...]


def _ffn(x2, g, wg, wu, wd, tm, tf):
    n, d = x2.shape
    ff = wg.shape[1]
    return pl.pallas_call(
        _ffn_kernel,
        out_shape=jax.ShapeDtypeStruct((n, d), F32),
        grid=(n // tm, ff // tf),
        in_specs=[pl.BlockSpec((tm, d), lambda i, f: (i, 0)),
                  pl.BlockSpec((1, d), lambda i, f: (0, 0)),
                  pl.BlockSpec((d, tf), lambda i, f: (0, f)),
                  pl.BlockSpec((d, tf), lambda i, f: (0, f)),
                  pl.BlockSpec((tf, d), lambda i, f: (f, 0))],
        out_specs=pl.BlockSpec((tm, d), lambda i, f: (i, 0)),
        scratch_shapes=[pltpu.VMEM((tm, d), BF16), pltpu.VMEM((tm, d), F32)],
        compiler_params=_params("parallel", "arbitrary"),
        name="ffn_dense",
    )(x2, g, wg, wu, wd)


def _top2_gates(logits):
    lane = lax.broadcasted_iota(jnp.int32, logits.shape, 1)
    lg = jnp.where(lane < N_EXPERTS, logits, -jnp.inf)
    m1 = jnp.max(lg, axis=1, keepdims=True)
    i1 = jnp.min(jnp.where(lg == m1, lane, LANES), axis=1, keepdims=True)
    lg2 = jnp.where(lane == i1, -jnp.inf, lg)
    m2 = jnp.max(lg2, axis=1, keepdims=True)
    i2 = jnp.min(jnp.where(lg2 == m2, lane, LANES), axis=1, keepdims=True)
    e = jnp.exp(m2 - m1)
    w1 = 1.0 / (1.0 + e)
    w2 = e / (1.0 + e)
    return jnp.where(lane == i1, w1, 0.0) + jnp.where(lane == i2, w2, 0.0)


def _moe_kernel(x_ref, g_ref, wr_ref, wg_ref, wu_ref, wd_ref, gf_ref, o_ref, h_ref, gates_ref, acc_ref):
    e = pl.program_id(1)
    f = pl.program_id(2)

    @pl.when((e == 0) & (f == 0))
    def _():
        x = x_ref[...]
        h = _rms(x, g_ref[...])
        h_ref[...] = h.astype(BF16)
        logits = jnp.dot(h, wr_ref[...], precision=lax.Precision.HIGHEST, preferred_element_type=F32)
        gates_ref[...] = _top2_gates(logits)
        acc_ref[...] = x

    h = h_ref[...]
    lane = lax.broadcasted_iota(jnp.int32, (1, LANES), 1)
    gate_e = jnp.sum(jnp.where(lane == e, gates_ref[...], 0.0), axis=1, keepdims=True)
    gate = jnp.dot(h, wg_ref[0], preferred_element_type=F32)
    up = jnp.dot(h, wu_ref[0], preferred_element_type=F32)
    act = (_silu(gate) * up * gate_e).astype(BF16)
    acc_ref[...] += jnp.dot(act, wd_ref[0], preferred_element_type=F32)

    @pl.when((e == pl.num_programs(1) - 1) & (f == pl.num_programs(2) - 1))
    def _():
        o_ref[...] = _rms(acc_ref[...], gf_ref[...])


def _moe(x2, g, w_router, wg, wu, wd, g_final, tm, tf):
    n, d = x2.shape
    n_exp, _, ff = wg.shape
    return pl.pallas_call(
        _moe_kernel,
        out_shape=jax.ShapeDtypeStruct((n, d), F32),
        grid=(n // tm, n_exp, ff // tf),
        in_specs=[pl.BlockSpec((tm, d), lambda i, e, f: (i, 0)),
                  pl.BlockSpec((1, d), lambda i, e, f: (0, 0)),
                  pl.BlockSpec((d, LANES), lambda i, e, f: (0, 0)),
                  pl.BlockSpec((1, d, tf), lambda i, e, f: (e, 0, f)),
                  pl.BlockSpec((1, d, tf), lambda i, e, f: (e, 0, f)),
                  pl.BlockSpec((1, tf, d), lambda i, e, f: (e, f, 0)),
                  pl.BlockSpec((1, d), lambda i, e, f: (0, 0))],
        out_specs=pl.BlockSpec((tm, d), lambda i, e, f: (i, 0)),
        scratch_shapes=[pltpu.VMEM((tm, d), BF16), pltpu.VMEM((tm, LANES), F32),
                        pltpu.VMEM((tm, d), F32)],
        compiler_params=_params("parallel", "arbitrary", "arbitrary"),
        name="moe_dense",
    )(x2, g, w_router, wg, wu, wd, g_final)


def _pad_cols(w, width):
    return jnp.pad(w, ((0, 0), (0, width - w.shape[1])))


def _prep_w_in(w):
    parts = []
    off = 0
    for size in IN_SIZES:
        parts.append(w[:, off:off + size])
        off += size
    cq, ckv, kr, q_b, k_b, v_b, q_c, k_c, v_c, f_c, q_d, k_d, v_d = parts
    d = w.shape[0]
    kr_blk = jnp.concatenate([jnp.zeros((d, MLA_NOPE), w.dtype), kr,
                              jnp.zeros((d, LANES - MLA_NOPE - MLA_ROPE), w.dtype)], axis=1)
    kr4 = jnp.tile(kr_blk, (1, GROUP_HEADS))
    cols = [cq, ckv, kr4, _pad_cols(f_c, LANES), q_b, k_b, v_b, q_c, k_c, v_c, q_d, k_d, v_d]
    return jnp.concatenate(cols, axis=1).astype(BF16)


def _prep_mla(w_uq, w_ukv):
    qk = MLA_NOPE + MLA_ROPE
    wuq = jnp.pad(w_uq.reshape(MLA_Q_RANK, GROUP_HEADS, qk), ((0, 0), (0, 0), (0, LANES - qk)))
    wuq = wuq.reshape(MLA_Q_RANK, GROUP_HEADS * LANES)
    kv = w_ukv.reshape(MLA_KV_RANK, GROUP_HEADS, MLA_NOPE + HEAD_DIM)
    wuk = jnp.pad(kv[:, :, :MLA_NOPE], ((0, 0), (0, 0), (0, LANES - MLA_NOPE)))
    wuk = wuk.reshape(MLA_KV_RANK, GROUP_HEADS * LANES)
    wuv = kv[:, :, MLA_NOPE:].reshape(MLA_KV_RANK, GROUP_WIDTH)
    return wuq.astype(BF16), wuk.astype(BF16), wuv.astype(BF16)


def _row(v, width=None):
    v = v.reshape(1, -1).astype(F32)
    return v if width is None else _pad_cols(v, width)


def kernel(x, g_mix, w_in, b_forget, g_q_lat, g_kv_lat, w_uq, w_ukv, lambda_q1, lambda_k1, lambda_q2,
           lambda_k2, g_diff_sub, w_out, g_ffn, w_ffn_gate, w_ffn_up, w_ffn_down, w_router, w_exp_gate,
           w_exp_up, w_exp_down, g_final):
    b, s_len, d = x.shape
    depth = g_mix.shape[0]
    assert d == D_MODEL and depth == 2, "kernel is specialised to the two-layer trunk"
    n = b * s_len
    t_attn = 256
    tm_in = min(512, s_len)
    tm_tok = min(1024, n)
    assert s_len % t_attn == 0 and s_len % tm_in == 0 and n % tm_tok == 0

    tabs = _rope_tables(s_len)
    log_mult = _dilated_log_multiplicity(t_attn)
    n_lm = log_mult.shape[0]

    for l in range(depth):
        wuq, wuk, wuv = _prep_mla(w_uq[l], w_ukv[l])
        zz, zf = _proj_in(x, _row(g_mix[l]), _prep_w_in(w_in[l]), _row(g_q_lat[l]), _row(g_kv_lat[l]),
                          wuq, wuk, wuv, tabs, tm_in)
        ccol, crow = _forget_cumsum(zf, _row(b_forget[l], LANES), tm_in)

        lam_init = 0.8 - 0.6 * math.exp(-0.3 * l)
        lamv = jnp.concatenate(
            [_row(v, LANES) for v in (lambda_q1[l], lambda_k1[l], lambda_q2[l], lambda_k2[l])]
            + [jnp.zeros((4, LANES), F32)], axis=0)
        gsub = jnp.tile(_row(g_diff_sub[l]), (1, LANES // HEAD_DIM))

        y_a = _attention(zz, "mla", t_attn, ZZ_QA, ZZ_KA, ZZ_VA)
        c0 = ZZ_QB
        y_b = _attention(zz, "dil", t_attn, c0, c0 + 256, c0 + 512, extra=(log_mult,),
                         extra_specs=(pl.BlockSpec((n_lm, t_attn, t_attn), lambda bi, g, qi: (0, 0, 0)),))
        c0 += 3 * GROUP_WIDTH
        y_c = _attention(zz, "fox", t_attn, c0, c0 + 256, c0 + 512, extra=(ccol, crow),
                         extra_specs=(pl.BlockSpec((1, t_attn, LANES), lambda bi, g, qi: (bi, qi, 0)),
                                      pl.BlockSpec((1, 8, s_len), lambda bi, g, qi: (bi, 0, 0))))
        c0 += 3 * GROUP_WIDTH
        y_d = _attention(zz, "diff", t_attn, c0, c0 + 256, c0 + 512, extra=(lamv, gsub),
                         extra_specs=(pl.BlockSpec((8, LANES), lambda bi, g, qi: (0, 0)),
                                      pl.BlockSpec((1, LANES), lambda bi, g, qi: (0, 0))),
                         lam_init=lam_init)

        ys = [y.reshape(n, GROUP_WIDTH) for y in (y_a, y_b, y_c, y_d)]
        x2 = _proj_out(x.reshape(n, d), ys, w_out[l].astype(BF16), tm_tok)

        if l % 2 == 0:
            i = l // 2
            x2 = _ffn(x2, _row(g_ffn[l]), w_ffn_gate[i].astype(BF16), w_ffn_up[i].astype(BF16),
                      w_ffn_down[i].astype(BF16), tm_tok, 256)
        else:
            i = l // 2
            x2 = _moe(x2, _row(g_ffn[l]), _pad_cols(w_router[i].astype(F32), LANES),
                      w_exp_gate[i].astype(BF16), w_exp_up[i].astype(BF16), w_exp_down[i].astype(BF16),
                      _row(g_final), tm_tok, 512)
        x = x2.reshape(b, s_len, d)
    return x
```

```python
import functools
import math

import jax
import jax.numpy as jnp
from jax import lax
from jax.experimental import pallas as pl
from jax.experimental.pallas import tpu as pltpu

F32 = jnp.float32
BF16 = jnp.bfloat16
I32 = jnp.int32

D_MODEL = 1024
HEAD_DIM = 64
GROUP_HEADS = 4
GROUP_WIDTH = GROUP_HEADS * HEAD_DIM
N_MIXERS = 4
MLA_Q_RANK = 256
MLA_KV_RANK = 128
MLA_NOPE = 64
MLA_ROPE = 32
DIL_CONFIGS = ((128, 1), (512, 4), (2048, 16))
DIL_MAX_WINDOW = max(w for w, _ in DIL_CONFIGS)
DIFF_QK = HEAD_DIM // 2
N_EXPERTS = 8
TOP_K = 2
ROPE_THETA = 10000.0
NORM_EPS = 1e-6
NEG_INF = -1e30
LOG2E = math.log2(math.e)
ACC_ROWS = HEAD_DIM + 16
LANES = 128
IN_SIZES = (MLA_Q_RANK, MLA_KV_RANK, MLA_ROPE,
            GROUP_WIDTH, GROUP_WIDTH, GROUP_WIDTH,
            GROUP_WIDTH, GROUP_WIDTH, GROUP_WIDTH, GROUP_HEADS,
            GROUP_WIDTH, GROUP_WIDTH, GROUP_WIDTH)

WP_CQ, WP_CKV, WP_KR4, WP_F = 0, 256, 384, 896
WP_QB = 1024
WP_WIDTH = WP_QB + 9 * GROUP_WIDTH

ZZ_QA, ZZ_KA = 0, 512
ZZ_QB = 1024
ZZ_WIDTH = ZZ_QB + 6 * GROUP_WIDTH

ROPE_A, ROPE_B, ROPE_D = 0, 1, 2

VMEM_LIMIT = 56 * 1024 * 1024

T_ATTN = 256
TM_PROJ = 512
TM_TOK = 1024
TF_DENSE = 256
T_SLOT = 512
T_CHUNK = 512
TF_EXPERT = 512


def _params(*sem):
    return pltpu.CompilerParams(dimension_semantics=sem, vmem_limit_bytes=VMEM_LIMIT)


def _rope_tables(s_len):
    pos = jnp.arange(s_len, dtype=F32)[:, None]
    lane = jnp.arange(LANES)[None, :]

    def kind(rel, width, active):
        half = width // 2
        idx = (rel % half).astype(F32)
        inv = ROPE_THETA ** (-idx / half)
        ang = pos * inv
        cos, sin = jnp.cos(ang), jnp.sin(ang)
        first = active & (rel < half)
        second = active & (rel >= half)
        c = jnp.where(active, cos, 1.0)
        sm = jnp.where(first, -sin, 0.0)
        sp = jnp.where(second, sin, 0.0)
        return [c, sm, sp]

    a_active = (lane >= MLA_NOPE) & (lane < MLA_NOPE + MLA_ROPE)
    tabs = (kind(jnp.where(a_active, lane - MLA_NOPE, 0), MLA_ROPE, a_active)
            + kind(lane % HEAD_DIM, HEAD_DIM, lane >= 0)
            + kind(lane % DIFF_QK, DIFF_QK, lane >= 0))
    return jnp.stack([jnp.broadcast_to(t, (s_len, LANES)).astype(F32) for t in tabs])


_ROPE_HALF = {ROPE_A: MLA_ROPE // 2, ROPE_B: HEAD_DIM // 2, ROPE_D: DIFF_QK // 2}


def _rope(blk, tab_ref, kind):
    half = _ROPE_HALF[kind]
    c = tab_ref[3 * kind]
    sm = tab_ref[3 * kind + 1]
    sp = tab_ref[3 * kind + 2]
    return blk * c + pltpu.roll(blk, LANES - half, 1) * sm + pltpu.roll(blk, half, 1) * sp


def _rms(x, g):
    return x * lax.rsqrt(jnp.mean(x * x, axis=-1, keepdims=True) + NORM_EPS) * g


def _proj_in_kernel(x_ref, g_ref, w_ref, gql_ref, gkvl_ref, wuq_ref, wuk_ref, wuv_ref, tab_ref,
                    zz_ref, vt_ref, zf_ref):
    h = _rms(x_ref[0], g_ref[...]).astype(BF16)

    def proj(c0, width):
        return jnp.dot(h, w_ref[:, c0:c0 + width], preferred_element_type=F32)

    def put(col, val):
        zz_ref[0, :, col:col + LANES] = val.astype(BF16)

    def put_v(mixer, val):
        vt_ref[0, mixer * GROUP_WIDTH:(mixer + 1) * GROUP_WIDTH, :] = val.T.astype(BF16)

    cqn = _rms(proj(WP_CQ, MLA_Q_RANK), gql_ref[...]).astype(BF16)
    qa = jnp.dot(cqn, wuq_ref[...], preferred_element_type=F32)
    ckvn = _rms(proj(WP_CKV, MLA_KV_RANK), gkvl_ref[...]).astype(BF16)
    ka = jnp.dot(ckvn, wuk_ref[...], preferred_element_type=F32) + proj(WP_KR4, GROUP_HEADS * LANES)
    scale_a = (MLA_NOPE + MLA_ROPE) ** -0.5 * LOG2E
    for hh in range(GROUP_HEADS):
        sl = slice(hh * LANES, (hh + 1) * LANES)
        put(ZZ_QA + hh * LANES, _rope(qa[:, sl], tab_ref, ROPE_A) * scale_a)
        put(ZZ_KA + hh * LANES, _rope(ka[:, sl], tab_ref, ROPE_A))
    put_v(0, jnp.dot(ckvn, wuv_ref[...], preferred_element_type=F32))

    zf_ref[0] = proj(WP_F, LANES)

    plan = (((ROPE_B, HEAD_DIM ** -0.5 * LOG2E), (ROPE_B, 1.0)),
            ((None, HEAD_DIM ** -0.5 * LOG2E), (None, 1.0)),
            ((ROPE_D, DIFF_QK ** -0.5 * LOG2E), (ROPE_D, 1.0)))
    for mx, qk_plan in enumerate(plan):
        for t, (kind, scale) in enumerate(qk_plan):
            z = proj(WP_QB + (3 * mx + t) * GROUP_WIDTH, GROUP_WIDTH)
            for half in range(GROUP_WIDTH // LANES):
                blk = z[:, half * LANES:(half + 1) * LANES]
                if kind is not None:
                    blk = _rope(blk, tab_ref, kind)
                if scale != 1.0:
                    blk = blk * scale
                put(ZZ_QB + (2 * mx + t) * GROUP_WIDTH + half * LANES, blk)
        put_v(mx + 1, proj(WP_QB + (3 * mx + 2) * GROUP_WIDTH, GROUP_WIDTH))


def _proj_in(x, g, w, gql, gkvl, wuq, wuk, wuv, tabs, tm):
    b, s_len, d = x.shape
    const = lambda shape: pl.BlockSpec(shape, lambda bi, si: (0,) * len(shape))
    return pl.pallas_call(
        _proj_in_kernel,
        out_shape=(jax.ShapeDtypeStruct((b, s_len, ZZ_WIDTH), BF16),
                   jax.ShapeDtypeStruct((b, N_MIXERS * GROUP_WIDTH, s_len), BF16),
                   jax.ShapeDtypeStruct((b, s_len, LANES), F32)),
        grid=(b, s_len // tm),
        in_specs=[
            pl.BlockSpec((1, tm, d), lambda bi, si: (bi, si, 0)),
            const((1, d)), const((d, WP_WIDTH)),
            const((1, MLA_Q_RANK)), const((1, MLA_KV_RANK)),
            const((MLA_Q_RANK, GROUP_HEADS * LANES)),
            const((MLA_KV_RANK, GROUP_HEADS * LANES)),
            const((MLA_KV_RANK, GROUP_WIDTH)),
            pl.BlockSpec((9, tm, LANES), lambda bi, si: (0, si, 0)),
        ],
        out_specs=(pl.BlockSpec((1, tm, ZZ_WIDTH), lambda bi, si: (bi, si, 0)),
                   pl.BlockSpec((1, N_MIXERS * GROUP_WIDTH, tm), lambda bi, si: (bi, 0, si)),
                   pl.BlockSpec((1, tm, LANES), lambda bi, si: (bi, si, 0))),
        compiler_params=_params("parallel", "parallel"),
        name="proj_in",
    )(x, g, w, gql, gkvl, wuq, wuk, wuv, tabs)


def _forget_cumsum_kernel(zf_ref, bias_ref, crep_ref, crow_ref, carry_ref):
    @pl.when(pl.program_id(1) == 0)
    def _():
        carry_ref[...] = jnp.zeros_like(carry_ref)

    t = zf_ref[0] + bias_ref[...]
    log_f = jnp.minimum(t, 0.0) - jnp.log(1.0 + jnp.exp(-jnp.abs(t)))
    tc = log_f.shape[0]
    row = lax.broadcasted_iota(I32, (tc, tc), 0)
    col = lax.broadcasted_iota(I32, (tc, tc), 1)
    tri = jnp.where(col <= row, 1.0, 0.0).astype(F32)
    cs = jnp.dot(tri, log_f, precision=lax.Precision.HIGHEST, preferred_element_type=F32) + carry_ref[...]
    carry_ref[...] = cs[tc - 1:tc, :]
    cs2 = cs * LOG2E
    crow_ref[0] = cs2.T[0:8, :]
    for hh in range(GROUP_HEADS):
        crep_ref[0, hh] = jnp.broadcast_to(cs2[:, hh:hh + 1], (tc, LANES))


def _forget_cumsum(zf, bias, tc):
    b, s_len, _ = zf.shape
    return pl.pallas_call(
        _forget_cumsum_kernel,
        out_shape=(jax.ShapeDtypeStruct((b, GROUP_HEADS, s_len, LANES), F32),
                   jax.ShapeDtypeStruct((b, 8, s_len), F32)),
        grid=(b, s_len // tc),
        in_specs=[pl.BlockSpec((1, tc, LANES), lambda bi, si: (bi, si, 0)),
                  pl.BlockSpec((1, LANES), lambda bi, si: (0, 0))],
        out_specs=(pl.BlockSpec((1, GROUP_HEADS, tc, LANES), lambda bi, si: (bi, 0, si, 0)),
                   pl.BlockSpec((1, 8, tc), lambda bi, si: (bi, 0, si))),
        scratch_shapes=[pltpu.VMEM((1, LANES), F32)],
        compiler_params=_params("parallel", "arbitrary"),
        name="forget_cumsum",
    )(zf, bias)


def _dilated_log2_multiplicity(t):
    n_tiles = DIL_MAX_WINDOW // t + 1
    k = jnp.arange(n_tiles)[:, None, None]
    j = jnp.arange(t)[None, :, None]
    i = jnp.arange(t)[None, None, :]
    delta = k * t + i - j
    mult = jnp.zeros(delta.shape, F32)
    for window, dilation in DIL_CONFIGS:
        mult = mult + ((delta >= 0) & (delta <= window) & (delta % dilation == 0)).astype(F32)
    return jnp.where(mult > 0, jnp.log2(jnp.maximum(mult, 1.0)), NEG_INF).astype(F32)


def _attn_kernel(*refs, mixer, t, lam_init):
    q_ref, k_ref, vt_ref = refs[:3]
    o_ref, acc_ref, sa_ref, sb_ref = refs[-4:]
    extra = refs[3:-4]
    qi = pl.program_id(1)
    sep = mixer == "mla"
    nsub = 2 * GROUP_HEADS if mixer == "diff" else GROUP_HEADS
    per_grp = 1 if sep else nsub // 2
    sub_w = LANES // per_grp

    lane = lax.broadcasted_iota(I32, (1, LANES), 1)
    q = q_ref[0]
    qs = []
    for hh in range(nsub):
        grp, pos = hh // per_grp, hh % per_grp
        qg = q[:, grp * LANES:(grp + 1) * LANES]
        if not sep:
            qg = jnp.where((lane >= pos * sub_w) & (lane < (pos + 1) * sub_w), qg, jnp.zeros_like(qg))
        qs.append(qg)

    if mixer == "fox":
        crep_ref, crow_ref = extra
        q_off = pl.multiple_of(qi * t, t)
        cq = [crow_ref[0, hh:hh + 1, pl.ds(q_off, t)] for hh in range(nsub)]
    if mixer == "dil":
        (lm_ref,) = extra

    key_idx = lax.broadcasted_iota(I32, (t, t), 0)
    qry_idx = lax.broadcasted_iota(I32, (t, t), 1)
    causal = key_idx <= qry_idx

    acc_ref[...] = jnp.zeros_like(acc_ref)

    def scores(j, s_ref):
        off = pl.multiple_of(j * t, t)
        k = k_ref[0, pl.ds(off, t), :]
        for hh in range(nsub):
            grp = hh // per_grp
            s_ref[hh] = lax.dot_general(k[:, grp * LANES:(grp + 1) * LANES], qs[hh],
                                        (((1,), (1,)), ((), ())), preferred_element_type=F32)

    def consume(j, s_ref, ms, diag):
        off = pl.multiple_of(j * t, t)
        ss = []
        for hh in range(nsub):
            s = s_ref[hh]
            if mixer == "fox":
                ck = crep_ref[0, hh, pl.ds(off, t), :]
                s = s + (cq[hh] - jnp.concatenate([ck] * (t // LANES), axis=1))
            if mixer == "dil":
                s = s + lm_ref[qi - j]
            elif diag:
                s = jnp.where(causal, s, NEG_INF)
            ss.append(s)
        new_m = [jnp.maximum(ms[hh], jnp.max(ss[hh], axis=0, keepdims=True)) for hh in range(nsub)]
        ps = [jnp.exp2(ss[hh] - new_m[hh]).astype(BF16) for hh in range(nsub)]
        alphas = [jnp.exp2(ms[hh] - new_m[hh]) for hh in range(nsub)]
        pvs = []
        for hh in range(nsub):
            vh = hh * GROUP_HEADS // nsub
            vt = vt_ref[0, vh * HEAD_DIM:(vh + 1) * HEAD_DIM, pl.ds(off, t)]
            vt1 = jnp.concatenate([vt, jnp.ones((ACC_ROWS - HEAD_DIM, t), BF16)], axis=0)
            pvs.append(jnp.dot(vt1, ps[hh], preferred_element_type=F32))
        for hh in range(nsub):
            acc_ref[hh] = acc_ref[hh] * alphas[hh] + pvs[hh]
        return tuple(new_m)

    def pair(i, ms):
        j = lo + 2 * i
        scores(j + 1, sb_ref)
        ms = consume(j, sa_ref, ms, False)
        scores(j + 2, sa_ref)
        return consume(j + 1, sb_ref, ms, False)

    lo = jnp.maximum(qi - DIL_MAX_WINDOW // t, 0) if mixer == "dil" else 0
    n_full = qi - lo
    init = tuple(jnp.full((1, t), NEG_INF, F32) for _ in range(nsub))
    scores(lo, sa_ref)
    ms = lax.fori_loop(0, n_full // 2, pair, init)

    def odd_tail(ms):
        scores(qi, sb_ref)
        ms = consume(qi - 1, sa_ref, ms, False)
        return consume(qi, sb_ref, ms, True)

    def even_tail(ms):
        return consume(qi, sa_ref, ms, True)

    lax.cond(n_full % 2 == 1, odd_tail, even_tail, ms)

    def head_out(hh):
        a = acc_ref[hh]
        return a[:HEAD_DIM] / a[HEAD_DIM:HEAD_DIM + 1]

    if mixer != "diff":
        o_t = jnp.concatenate([head_out(hh) for hh in range(nsub)], axis=0)
        o_ref[0] = o_t.T.astype(o_ref.dtype)
    else:
        lamv_ref, gsub_ref = extra
        lamv = lamv_ref[...]
        e1 = jnp.exp(jnp.sum(lamv[0:1] * lamv[1:2], axis=1, keepdims=True))
        e2 = jnp.exp(jnp.sum(lamv[2:3] * lamv[3:4], axis=1, keepdims=True))
        lam = e1 - e2 + lam_init
        heads = []
        for vh in range(GROUP_HEADS):
            o = head_out(2 * vh) - lam * head_out(2 * vh + 1)
            ms_h = jnp.mean(o * o, axis=0, keepdims=True)
            heads.append(o * lax.rsqrt(ms_h + NORM_EPS))
        o_t = jnp.concatenate(heads, axis=0)
        o_ref[0] = (o_t.T * gsub_ref[...] * (1.0 - lam_init)).astype(o_ref.dtype)


def _attention(zz, vt, mixer_idx, mixer, t, q_col, k_col, extra=(), extra_specs=(), lam_init=0.0):
    b, s_len, _ = zz.shape
    qw = GROUP_HEADS * LANES if mixer == "mla" else GROUP_WIDTH
    qb, kb = q_col // qw, k_col // qw
    nsub = 2 * GROUP_HEADS if mixer == "diff" else GROUP_HEADS
    return pl.pallas_call(
        functools.partial(_attn_kernel, mixer=mixer, t=t, lam_init=lam_init),
        out_shape=jax.ShapeDtypeStruct((b, s_len, GROUP_WIDTH), BF16),
        grid=(b, s_len // t),
        in_specs=[pl.BlockSpec((1, t, qw), lambda bi, qi: (bi, qi, qb)),
                  pl.BlockSpec((1, s_len, qw), lambda bi, qi: (bi, 0, kb)),
                  pl.BlockSpec((1, GROUP_WIDTH, s_len), lambda bi, qi: (bi, mixer_idx, 0)),
                  *extra_specs],
        out_specs=pl.BlockSpec((1, t, GROUP_WIDTH), lambda bi, qi: (bi, qi, 0)),
        scratch_shapes=[pltpu.VMEM((nsub, ACC_ROWS, t), F32), pltpu.VMEM((nsub, t, t), F32),
                        pltpu.VMEM((nsub, t, t), F32)],
        compiler_params=_params("parallel", "arbitrary"),
        name="attn_" + mixer,
    )(zz, zz, vt, *extra)


def _proj_out_kernel(x_ref, ya_ref, yb_ref, yc_ref, yd_ref, w_ref, o_ref):
    acc = x_ref[...]
    for i, y_ref in enumerate((ya_ref, yb_ref, yc_ref, yd_ref)):
        acc = acc + jnp.dot(y_ref[...], w_ref[i * GROUP_WIDTH:(i + 1) * GROUP_WIDTH, :],
                            preferred_element_type=F32)
    o_ref[...] = acc


def _proj_out(x2, ys, w, tm):
    n, d = x2.shape
    yspec = pl.BlockSpec((tm, GROUP_WIDTH), lambda i: (i, 0))
    return pl.pallas_call(
        _proj_out_kernel,
        out_shape=jax.ShapeDtypeStruct((n, d), F32),
        grid=(n // tm,),
        in_specs=[pl.BlockSpec((tm, d), lambda i: (i, 0)), yspec, yspec, yspec, yspec,
                  pl.BlockSpec((4 * GROUP_WIDTH, d), lambda i: (0, 0))],
        out_specs=pl.BlockSpec((tm, d), lambda i: (i, 0)),
        compiler_params=_params("parallel"),
        name="proj_out",
    )(x2, *ys, w)


def _silu(gate):
    return gate / (1.0 + jnp.exp(-gate))


def _ffn_kernel(x_ref, g_ref, wg_ref, wu_ref, wd_ref, o_ref, h_ref, acc_ref):
    f = pl.program_id(1)

    @pl.when(f == 0)
    def _():
        x = x_ref[...]
        h_ref[...] = _rms(x, g_ref[...]).astype(BF16)
        acc_ref[...] = x

    h = h_ref[...]
    gate = jnp.dot(h, wg_ref[...], preferred_element_type=F32)
    up = jnp.dot(h, wu_ref[...], preferred_element_type=F32)
    act = (_silu(gate) * up).astype(BF16)
    acc_ref[...] += jnp.dot(act, wd_ref[...], preferred_element_type=F32)

    @pl.when(f == pl.num_programs(1) - 1)
    def _():
        o_ref[...] = acc_ref[...]


def _ffn(x2, g, wg, wu, wd, tm, tf):
    n, d = x2.shape
    ff = wg.shape[1]
    return pl.pallas_call(
        _ffn_kernel,
        out_shape=jax.ShapeDtypeStruct((n, d), F32),
        grid=(n // tm, ff // tf),
        in_specs=[pl.BlockSpec((tm, d), lambda i, f: (i, 0)),
                  pl.BlockSpec((1, d), lambda i, f: (0, 0)),
                  pl.BlockSpec((d, tf), lambda i, f: (0, f)),
                  pl.BlockSpec((d, tf), lambda i, f: (0, f)),
                  pl.BlockSpec((tf, d), lambda i, f: (f, 0))],
        out_specs=pl.BlockSpec((tm, d), lambda i, f: (i, 0)),
        scratch_shapes=[pltpu.VMEM((tm, d), BF16), pltpu.VMEM((tm, d), F32)],
        compiler_params=_params("parallel", "arbitrary"),
        name="ffn_dense",
    )(x2, g, wg, wu, wd)


def _router_kernel(x_ref, g_ref, wr_ref, h_ref, route_ref):
    h = _rms(x_ref[...], g_ref[...])
    h_ref[...] = h.astype(BF16)
    logits = jnp.dot(h, wr_ref[...], precision=lax.Precision.HIGHEST, preferred_element_type=F32)
    lane = lax.broadcasted_iota(I32, logits.shape, 1)
    lg = jnp.where(lane < N_EXPERTS, logits, -jnp.inf)
    m1 = jnp.max(lg, axis=1, keepdims=True)
    i1 = jnp.min(jnp.where(lg == m1, lane, LANES), axis=1, keepdims=True)
    lg2 = jnp.where(lane == i1, -jnp.inf, lg)
    m2 = jnp.max(lg2, axis=1, keepdims=True)
    i2 = jnp.min(jnp.where(lg2 == m2, lane, LANES), axis=1, keepdims=True)
    e = jnp.exp(m2 - m1)
    w1 = 1.0 / (1.0 + e)
    w2 = e / (1.0 + e)
    route_ref[...] = jnp.where(lane == 0, i1.astype(F32),
                               jnp.where(lane == 1, i2.astype(F32),
                                         jnp.where(lane == 2, w1, jnp.where(lane == 3, w2, 0.0))))


def _router(x2, g, w_router, tm):
    n, d = x2.shape
    return pl.pallas_call(
        _router_kernel,
        out_shape=(jax.ShapeDtypeStruct((n, d), BF16), jax.ShapeDtypeStruct((n, LANES), F32)),
        grid=(n // tm,),
        in_specs=[pl.BlockSpec((tm, d), lambda i: (i, 0)),
                  pl.BlockSpec((1, d), lambda i: (0, 0)),
                  pl.BlockSpec((d, LANES), lambda i: (0, 0))],
        out_specs=(pl.BlockSpec((tm, d), lambda i: (i, 0)), pl.BlockSpec((tm, LANES), lambda i: (i, 0))),
        compiler_params=_params("parallel"),
        name="moe_router",
    )(x2, g, w_router)


def _dispatch_kernel(vtile_ref, vchunk_ref, vfirst_ref, nvis_ref, pos1_ref, pos2_ref, h_ref, o_ref):
    v = pl.program_id(0)

    @pl.when(v < nvis_ref[0])
    def _():
        ts, tc = o_ref.shape[0], h_ref.shape[0]
        slot = vtile_ref[v] * ts + lax.broadcasted_iota(I32, (ts, tc), 0)
        hit = jnp.where(slot == pos1_ref[...], 1.0, jnp.where(slot == pos2_ref[...], 1.0, 0.0))
        rows = jnp.dot(hit.astype(BF16), h_ref[...], preferred_element_type=F32).astype(o_ref.dtype)

        @pl.when(vfirst_ref[v] == 1)
        def _():
            o_ref[...] = rows

        @pl.when(vfirst_ref[v] == 0)
        def _():
            o_ref[...] = o_ref[...] + rows


def _dispatch(h, pos1, pos2, visits, n_slots, ts, tc):
    n, d = h.shape
    vtile, vchunk, vfirst, nvis = visits
    grid_spec = pltpu.PrefetchScalarGridSpec(
        num_scalar_prefetch=4,
        grid=(vtile.shape[0],),
        in_specs=[pl.BlockSpec((1, tc), lambda v, vt, vc, vf, nv: (0, vc[v])),
                  pl.BlockSpec((1, tc), lambda v, vt, vc, vf, nv: (0, vc[v])),
                  pl.BlockSpec((tc, d), lambda v, vt, vc, vf, nv: (vc[v], 0))],
        out_specs=pl.BlockSpec((ts, d), lambda v, vt, vc, vf, nv: (vt[v], 0)),
    )
    return pl.pallas_call(
        _dispatch_kernel,
        out_shape=jax.ShapeDtypeStruct((n_slots, d), BF16),
        grid_spec=grid_spec,
        compiler_params=_params("arbitrary"),
        name="moe_dispatch",
    )(vtile, vchunk, vfirst, nvis, pos1, pos2, h)


def _expert_ffn_kernel(texp_ref, nused_ref, xs_ref, wg_ref, wu_ref, wd_ref, o_ref, acc_ref):
    i = pl.program_id(0)
    f = pl.program_id(1)

    @pl.when(f == 0)
    def _():
        acc_ref[...] = jnp.zeros_like(acc_ref)

    @pl.when(i < nused_ref[0])
    def _():
        xs = xs_ref[...]
        gate = jnp.dot(xs, wg_ref[0], preferred_element_type=F32)
        up = jnp.dot(xs, wu_ref[0], preferred_element_type=F32)
        act = (_silu(gate) * up).astype(BF16)
        acc_ref[...] += jnp.dot(act, wd_ref[0], preferred_element_type=F32)

    @pl.when(f == pl.num_programs(1) - 1)
    def _():
        o_ref[...] = acc_ref[...].astype(o_ref.dtype)


def _expert_ffn(xs, tile_expert, n_used, wg, wu, wd, ts, tf):
    n_slots, d = xs.shape
    ff = wg.shape[2]

    def fchunk(i, f, nu):
        return jnp.where(i < nu[0], f, 0)

    grid_spec = pltpu.PrefetchScalarGridSpec(
        num_scalar_prefetch=2,
        grid=(n_slots // ts, ff // tf),
        in_specs=[pl.BlockSpec((ts, d), lambda i, f, te, nu: (i, 0)),
                  pl.BlockSpec((1, d, tf), lambda i, f, te, nu: (te[i], 0, fchunk(i, f, nu))),
                  pl.BlockSpec((1, d, tf), lambda i, f, te, nu: (te[i], 0, fchunk(i, f, nu))),
                  pl.BlockSpec((1, tf, d), lambda i, f, te, nu: (te[i], fchunk(i, f, nu), 0))],
        out_specs=pl.BlockSpec((ts, d), lambda i, f, te, nu: (i, 0)),
        scratch_shapes=[pltpu.VMEM((ts, d), F32)],
    )
    return pl.pallas_call(
        _expert_ffn_kernel,
        out_shape=jax.ShapeDtypeStruct((n_slots, d), BF16),
        grid_spec=grid_spec,
        compiler_params=_params("arbitrary", "arbitrary"),
        name="moe_expert_ffn",
    )(tile_expert, n_used, xs, wg, wu, wd)


def _combine_kernel(vchunk_ref, vtile_ref, vfirst_ref, vlast_ref, nvis_ref,
                    stok_ref, sgate_ref, ys_ref, x_ref, gf_ref, o_ref, acc_ref):
    v = pl.program_id(0)

    @pl.when(v < nvis_ref[0])
    def _():
        @pl.when(vfirst_ref[v] == 1)
        def _():
            acc_ref[...] = x_ref[...]

        tc, ts = x_ref.shape[0], ys_ref.shape[0]
        tok = vchunk_ref[v] * tc + lax.broadcasted_iota(I32, (tc, ts), 0)
        gmat = jnp.where(tok == stok_ref[...], sgate_ref[...], 0.0).astype(BF16)
        acc_ref[...] += jnp.dot(gmat, ys_ref[...], preferred_element_type=F32)

        @pl.when(vlast_ref[v] == 1)
        def _():
            o_ref[...] = _rms(acc_ref[...], gf_ref[...])


def _combine(x2, ys, slot_token, slot_gate, g_final, visits, ts, tc):
    n, d = x2.shape
    vchunk, vtile, vfirst, vlast, nvis = visits
    grid_spec = pltpu.PrefetchScalarGridSpec(
        num_scalar_prefetch=5,
        grid=(vchunk.shape[0],),
        in_specs=[pl.BlockSpec((1, ts), lambda v, vc, vt, vf, vl, nv: (0, vt[v])),
                  pl.BlockSpec((1, ts), lambda v, vc, vt, vf, vl, nv: (0, vt[v])),
                  pl.BlockSpec((ts, d), lambda v, vc, vt, vf, vl, nv: (vt[v], 0)),
                  pl.BlockSpec((tc, d), lambda v, vc, vt, vf, vl, nv: (vc[v], 0)),
                  pl.BlockSpec((1, d), lambda v, vc, vt, vf, vl, nv: (0, 0))],
        out_specs=pl.BlockSpec((tc, d), lambda v, vc, vt, vf, vl, nv: (vc[v], 0)),
        scratch_shapes=[pltpu.VMEM((tc, d), F32)],
    )
    return pl.pallas_call(
        _combine_kernel,
        out_shape=jax.ShapeDtypeStruct((n, d), F32),
        grid_spec=grid_spec,
        compiler_params=_params("arbitrary"),
        name="moe_combine",
    )(vchunk, vtile, vfirst, vlast, nvis, slot_token, slot_gate, ys, x2, g_final)


def _flatten_visits(counts, n_max):
    ends = jnp.cumsum(counts)
    total = ends[-1]
    v = jnp.minimum(jnp.arange(n_max, dtype=I32), total - 1)
    seg = jnp.searchsorted(ends, v, side="right").astype(I32)
    start = ends[seg] - counts[seg]
    within = v - start
    return seg, within, (within == 0).astype(I32), (within == counts[seg] - 1).astype(I32), total.astype(I32)


def _moe_top2(x2, g, w_router, wg, wu, wd, g_final):
    n, d = x2.shape
    ts, tc = T_SLOT, min(T_CHUNK, n)
    n_chunks = n // tc
    n_tiles = TOP_K * n // ts + N_EXPERTS
    n_slots = n_tiles * ts

    h, route = _router(x2, g, w_router, min(TM_TOK, n))
    e1, e2 = route[:, 0].astype(I32), route[:, 1].astype(I32)
    w1, w2 = route[:, 2], route[:, 3]

    experts = jnp.arange(N_EXPERTS, dtype=I32)
    hit = ((e1[:, None] == experts) | (e2[:, None] == experts)).astype(I32)
    before = jnp.cumsum(hit, axis=0) - hit
    counts = jnp.sum(hit, axis=0)
    padded = (counts + ts - 1) // ts * ts
    group_end = jnp.cumsum(padded)
    group_start = group_end - padded
    tok = jnp.arange(n, dtype=I32)
    pos1 = group_start[e1] + before[tok, e1]
    pos2 = group_start[e2] + before[tok, e2]
    n_used = (group_end[-1] // ts).astype(I32)
    tile_expert = jnp.minimum(jnp.searchsorted(group_end // ts, jnp.arange(n_tiles, dtype=I32), side="right"),
                              N_EXPERTS - 1).astype(I32)
    slot_token = jnp.full((n_slots,), -1, I32).at[pos1].set(tok).at[pos2].set(tok)
    slot_gate = jnp.zeros((n_slots,), F32).at[pos1].set(w1).at[pos2].set(w2)

    st = slot_token.reshape(n_tiles, ts)
    has = jnp.any(st >= 0, axis=1)
    c_lo = jnp.where(has, jnp.min(jnp.where(st >= 0, st, n), axis=1) // tc, 0).astype(I32)
    c_hi = jnp.where(has, jnp.max(st, axis=1) // tc, 0).astype(I32)
    d_tile, d_within, d_first, _, d_total = _flatten_visits(c_hi - c_lo + 1, n_tiles + N_EXPERTS * n_chunks)
    d_visits = (d_tile, c_lo[d_tile] + d_within, d_first, d_total.reshape(1))

    in_e = hit.reshape(n_chunks, tc, N_EXPERTS) > 0
    pos_e = jnp.where(e1[:, None] == experts, pos1[:, None], pos2[:, None]).reshape(n_chunks, tc, N_EXPERTS)
    any_e = jnp.any(in_e, axis=1)
    t_lo = jnp.where(any_e, jnp.min(jnp.where(in_e, pos_e, n_slots), axis=1) // ts, 0).astype(I32)
    t_hi = jnp.where(any_e, jnp.max(jnp.where(in_e, pos_e, -1), axis=1) // ts, -1).astype(I32)
    seg_counts = (t_hi - t_lo + 1).reshape(-1)
    c_seg, c_within, c_first_seg, c_last_seg, c_total = _flatten_visits(seg_counts, n_tiles + N_EXPERTS * n_chunks)
    c_chunk = c_seg // N_EXPERTS
    c_tile = t_lo.reshape(-1)[c_seg] + c_within
    prev_chunk = jnp.concatenate([jnp.full((1,), -1, I32), c_chunk[:-1]])
    next_chunk = jnp.concatenate([c_chunk[1:], jnp.full((1,), -1, I32)])
    vid = jnp.arange(c_chunk.shape[0], dtype=I32)
    c_first = (c_chunk != prev_chunk).astype(I32)
    c_last = ((c_chunk != next_chunk) | (vid == c_total - 1)).astype(I32)
    c_visits = (c_chunk, c_tile, c_first, c_last, c_total.reshape(1))

    xs = _dispatch(h, pos1.reshape(1, n), pos2.reshape(1, n), d_visits, n_slots, ts, tc)
    ys = _expert_ffn(xs, tile_expert, n_used.reshape(1), wg, wu, wd, ts, TF_EXPERT)
    return _combine(x2, ys, slot_token.reshape(1, n_slots), slot_gate.reshape(1, n_slots), g_final,
                    c_visits, ts, tc)


def _pad_cols(w, width):
    return jnp.pad(w, ((0, 0), (0, width - w.shape[1])))


def _prep_w_in(w):
    parts = []
    off = 0
    for size in IN_SIZES:
        parts.append(w[:, off:off + size])
        off += size
    cq, ckv, kr, q_b, k_b, v_b, q_c, k_c, v_c, f_c, q_d, k_d, v_d = parts
    d = w.shape[0]
    kr_blk = jnp.concatenate([jnp.zeros((d, MLA_NOPE), w.dtype), kr,
                              jnp.zeros((d, LANES - MLA_NOPE - MLA_ROPE), w.dtype)], axis=1)
    kr4 = jnp.tile(kr_blk, (1, GROUP_HEADS))
    cols = [cq, ckv, kr4, _pad_cols(f_c, LANES), q_b, k_b, v_b, q_c, k_c, v_c, q_d, k_d, v_d]
    return jnp.concatenate(cols, axis=1).astype(BF16)


def _prep_mla(w_uq, w_ukv):
    qk = MLA_NOPE + MLA_ROPE
    wuq = jnp.pad(w_uq.reshape(MLA_Q_RANK, GROUP_HEADS, qk), ((0, 0), (0, 0), (0, LANES - qk)))
    wuq = wuq.reshape(MLA_Q_RANK, GROUP_HEADS * LANES)
    kv = w_ukv.reshape(MLA_KV_RANK, GROUP_HEADS, MLA_NOPE + HEAD_DIM)
    wuk = jnp.pad(kv[:, :, :MLA_NOPE], ((0, 0), (0, 0), (0, LANES - MLA_NOPE)))
    wuk = wuk.reshape(MLA_KV_RANK, GROUP_HEADS * LANES)
    wuv = kv[:, :, MLA_NOPE:].reshape(MLA_KV_RANK, GROUP_WIDTH)
    return wuq.astype(BF16), wuk.astype(BF16), wuv.astype(BF16)


def _row(v, width=None):
    v = v.reshape(1, -1).astype(F32)
    return v if width is None else _pad_cols(v, width)


def kernel(x, g_mix, w_in, b_forget, g_q_lat, g_kv_lat, w_uq, w_ukv, lambda_q1, lambda_k1, lambda_q2,
           lambda_k2, g_diff_sub, w_out, g_ffn, w_ffn_gate, w_ffn_up, w_ffn_down, w_router, w_exp_gate,
           w_exp_up, w_exp_down, g_final):
    b, s_len, d = x.shape
    depth = g_mix.shape[0]
    assert d == D_MODEL and depth == 2, "kernel is specialised to the two-layer trunk"
    assert w_router.shape[-1] == N_EXPERTS
    n = b * s_len
    t_attn = T_ATTN
    tm_in = min(TM_PROJ, s_len)
    tm_tok = min(TM_TOK, n)
    assert s_len % t_attn == 0 and s_len % tm_in == 0 and n % tm_tok == 0
    assert n % min(T_CHUNK, n) == 0 and (TOP_K * n) % T_SLOT == 0

    tabs = _rope_tables(s_len)
    log_mult = _dilated_log2_multiplicity(t_attn)
    n_lm = log_mult.shape[0]

    for l in range(depth):
        wuq, wuk, wuv = _prep_mla(w_uq[l], w_ukv[l])
        zz, vt, zf = _proj_in(x, _row(g_mix[l]), _prep_w_in(w_in[l]), _row(g_q_lat[l]),
                              _row(g_kv_lat[l]), wuq, wuk, wuv, tabs, tm_in)
        crep, crow = _forget_cumsum(zf, _row(b_forget[l], LANES), tm_in)

        lam_init = 0.8 - 0.6 * math.exp(-0.3 * l)
        lamv = jnp.concatenate(
            [_row(v, LANES) for v in (lambda_q1[l], lambda_k1[l], lambda_q2[l], lambda_k2[l])]
            + [jnp.zeros((4, LANES), F32)], axis=0)
        gsub = jnp.tile(_row(g_diff_sub[l]), (1, GROUP_HEADS))

        y_a = _attention(zz, vt, 0, "mla", t_attn, ZZ_QA, ZZ_KA)
        c0 = ZZ_QB
        y_b = _attention(zz, vt, 1, "dil", t_attn, c0, c0 + GROUP_WIDTH, extra=(log_mult,),
                         extra_specs=(pl.BlockSpec((n_lm, t_attn, t_attn), lambda bi, qi: (0, 0, 0)),))
        c0 += 2 * GROUP_WIDTH
        y_c = _attention(zz, vt, 2, "fox", t_attn, c0, c0 + GROUP_WIDTH, extra=(crep, crow),
                         extra_specs=(pl.BlockSpec((1, GROUP_HEADS, s_len, LANES), lambda bi, qi: (bi, 0, 0, 0)),
                                      pl.BlockSpec((1, 8, s_len), lambda bi, qi: (bi, 0, 0))))
        c0 += 2 * GROUP_WIDTH
        y_d = _attention(zz, vt, 3, "diff", t_attn, c0, c0 + GROUP_WIDTH, extra=(lamv, gsub),
                         extra_specs=(pl.BlockSpec((8, LANES), lambda bi, qi: (0, 0)),
                                      pl.BlockSpec((1, GROUP_WIDTH), lambda bi, qi: (0, 0))),
                         lam_init=lam_init)

        ys = [y.reshape(n, GROUP_WIDTH) for y in (y_a, y_b, y_c, y_d)]
        x2 = _proj_out(x.reshape(n, d), ys, w_out[l].astype(BF16), tm_tok)

        i = l // 2
        if l % 2 == 0:
            x2 = _ffn(x2, _row(g_ffn[l]), w_ffn_gate[i].astype(BF16), w_ffn_up[i].astype(BF16),
                      w_ffn_down[i].astype(BF16), tm_tok, TF_DENSE)
        else:
            x2 = _moe_top2(x2, _row(g_ffn[l]), _pad_cols(w_router[i].astype(F32), LANES),
                           w_exp_gate[i].astype(BF16), w_exp_up[i].astype(BF16),
                           w_exp_down[i].astype(BF16), _row(g_final))
        x = x2.reshape(b, s_len, d)
    return x
```

```python
import functools
import math

import jax
import jax.numpy as jnp
from jax import lax
from jax.experimental import pallas as pl
from jax.experimental.pallas import tpu as pltpu

F32 = jnp.float32
BF16 = jnp.bfloat16
I32 = jnp.int32

D_MODEL = 1024
HEAD_DIM = 64
GROUP_HEADS = 4
GROUP_WIDTH = GROUP_HEADS * HEAD_DIM
N_MIXERS = 4
MLA_Q_RANK = 256
MLA_KV_RANK = 128
MLA_NOPE = 64
MLA_ROPE = 32
DIL_CONFIGS = ((128, 1), (512, 4), (2048, 16))
DIL_MAX_WINDOW = max(w for w, _ in DIL_CONFIGS)
DIFF_QK = HEAD_DIM // 2
N_EXPERTS = 8
TOP_K = 2
ROPE_THETA = 10000.0
NORM_EPS = 1e-6
NEG_INF = -1e30
LOG2E = math.log2(math.e)
ACC_ROWS = HEAD_DIM + 16
LANES = 128
IN_SIZES = (MLA_Q_RANK, MLA_KV_RANK, MLA_ROPE,
            GROUP_WIDTH, GROUP_WIDTH, GROUP_WIDTH,
            GROUP_WIDTH, GROUP_WIDTH, GROUP_WIDTH, GROUP_HEADS,
            GROUP_WIDTH, GROUP_WIDTH, GROUP_WIDTH)

WP_CQ, WP_CKV, WP_KR, WP_F = 0, 256, 384, 512
WP_QB = 640
WP_WIDTH = WP_QB + 9 * GROUP_WIDTH

ZZ_QA, ZZ_KA = 0, 512
ZZ_QB = 1024
ZZ_WIDTH = ZZ_QB + 6 * GROUP_WIDTH

ROPE_A, ROPE_B, ROPE_D = 0, 1, 2

VMEM_LIMIT = 56 * 1024 * 1024

T_ATTN = 256
TM_PROJ = 512
TM_TOK = 1024
TF_DENSE = 256
T_SLOT = 512
T_CHUNK = 512
TF_EXPERT = 512
SUB_DISPATCH = 128
SUB_COMBINE = 256


def _params(*sem):
    return pltpu.CompilerParams(dimension_semantics=sem, vmem_limit_bytes=VMEM_LIMIT)


def _rope_tables(s_len):
    pos = jnp.arange(s_len, dtype=F32)[:, None]
    lane = jnp.arange(LANES)[None, :]

    def kind(rel, width, active):
        half = width // 2
        idx = (rel % half).astype(F32)
        inv = ROPE_THETA ** (-idx / half)
        ang = pos * inv
        cos, sin = jnp.cos(ang), jnp.sin(ang)
        first = active & (rel < half)
        second = active & (rel >= half)
        c = jnp.where(active, cos, 1.0)
        sm = jnp.where(first, -sin, 0.0)
        sp = jnp.where(second, sin, 0.0)
        return [c, sm, sp]

    a_active = (lane >= MLA_NOPE) & (lane < MLA_NOPE + MLA_ROPE)
    tabs = (kind(jnp.where(a_active, lane - MLA_NOPE, 0), MLA_ROPE, a_active)
            + kind(lane % HEAD_DIM, HEAD_DIM, lane >= 0)
            + kind(lane % DIFF_QK, DIFF_QK, lane >= 0))
    return jnp.stack([jnp.broadcast_to(t, (s_len, LANES)).astype(F32) for t in tabs])


_ROPE_HALF = {ROPE_A: MLA_ROPE // 2, ROPE_B: HEAD_DIM // 2, ROPE_D: DIFF_QK // 2}


def _rope(blk, tab_ref, kind):
    half = _ROPE_HALF[kind]
    c = tab_ref[3 * kind]
    sm = tab_ref[3 * kind + 1]
    sp = tab_ref[3 * kind + 2]
    return blk * c + pltpu.roll(blk, LANES - half, 1) * sm + pltpu.roll(blk, half, 1) * sp


def _rms(x, g):
    return x * lax.rsqrt(jnp.mean(x * x, axis=-1, keepdims=True) + NORM_EPS) * g


def _proj_in_kernel(x_ref, g_ref, w_ref, gql_ref, gkvl_ref, wuq_ref, wuk_ref, wuv_ref, tab_ref,
                    zz_ref, vt_ref, zf_ref):
    h = _rms(x_ref[0], g_ref[...]).astype(BF16)

    def proj(c0, width):
        return jnp.dot(h, w_ref[:, c0:c0 + width], preferred_element_type=F32)

    def put(col, val):
        zz_ref[0, :, col:col + LANES] = val.astype(BF16)

    def put_v(mixer, val):
        vt_ref[0, mixer * GROUP_WIDTH:(mixer + 1) * GROUP_WIDTH, :] = val.T.astype(BF16)

    cqn = _rms(proj(WP_CQ, MLA_Q_RANK), gql_ref[...]).astype(BF16)
    qa = jnp.dot(cqn, wuq_ref[...], preferred_element_type=F32)
    ckvn = _rms(proj(WP_CKV, MLA_KV_RANK), gkvl_ref[...]).astype(BF16)
    ka = jnp.dot(ckvn, wuk_ref[...], preferred_element_type=F32)
    kr = proj(WP_KR, LANES)
    scale_a = (MLA_NOPE + MLA_ROPE) ** -0.5 * LOG2E
    for hh in range(GROUP_HEADS):
        sl = slice(hh * LANES, (hh + 1) * LANES)
        put(ZZ_QA + hh * LANES, _rope(qa[:, sl], tab_ref, ROPE_A) * scale_a)
        put(ZZ_KA + hh * LANES, _rope(ka[:, sl] + kr, tab_ref, ROPE_A))
    put_v(0, jnp.dot(ckvn, wuv_ref[...], preferred_element_type=F32))

    zf_ref[0] = proj(WP_F, LANES)

    plan = (((ROPE_B, HEAD_DIM ** -0.5 * LOG2E), (ROPE_B, 1.0)),
            ((None, HEAD_DIM ** -0.5 * LOG2E), (None, 1.0)),
            ((ROPE_D, DIFF_QK ** -0.5 * LOG2E), (ROPE_D, 1.0)))
    for mx, qk_plan in enumerate(plan):
        for t, (kind, scale) in enumerate(qk_plan):
            z = proj(WP_QB + (3 * mx + t) * GROUP_WIDTH, GROUP_WIDTH)
            for half in range(GROUP_WIDTH // LANES):
                blk = z[:, half * LANES:(half + 1) * LANES]
                if kind is not None:
                    blk = _rope(blk, tab_ref, kind)
                if scale != 1.0:
                    blk = blk * scale
                put(ZZ_QB + (2 * mx + t) * GROUP_WIDTH + half * LANES, blk)
        put_v(mx + 1, proj(WP_QB + (3 * mx + 2) * GROUP_WIDTH, GROUP_WIDTH))


def _proj_in(x, g, w, gql, gkvl, wuq, wuk, wuv, tabs, tm):
    b, s_len, d = x.shape
    const = lambda shape: pl.BlockSpec(shape, lambda bi, si: (0,) * len(shape))
    return pl.pallas_call(
        _proj_in_kernel,
        out_shape=(jax.ShapeDtypeStruct((b, s_len, ZZ_WIDTH), BF16),
                   jax.ShapeDtypeStruct((b, N_MIXERS * GROUP_WIDTH, s_len), BF16),
                   jax.ShapeDtypeStruct((b, s_len, LANES), F32)),
        grid=(b, s_len // tm),
        in_specs=[
            pl.BlockSpec((1, tm, d), lambda bi, si: (bi, si, 0)),
            const((1, d)), const((d, WP_WIDTH)),
            const((1, MLA_Q_RANK)), const((1, MLA_KV_RANK)),
            const((MLA_Q_RANK, GROUP_HEADS * LANES)),
            const((MLA_KV_RANK, GROUP_HEADS * LANES)),
            const((MLA_KV_RANK, GROUP_WIDTH)),
            pl.BlockSpec((9, tm, LANES), lambda bi, si: (0, si, 0)),
        ],
        out_specs=(pl.BlockSpec((1, tm, ZZ_WIDTH), lambda bi, si: (bi, si, 0)),
                   pl.BlockSpec((1, N_MIXERS * GROUP_WIDTH, tm), lambda bi, si: (bi, 0, si)),
                   pl.BlockSpec((1, tm, LANES), lambda bi, si: (bi, si, 0))),
        compiler_params=_params("parallel", "parallel"),
        name="proj_in",
    )(x, g, w, gql, gkvl, wuq, wuk, wuv, tabs)


def _forget_cumsum_kernel(zf_ref, bias_ref, crep_ref, crow_ref, carry_ref):
    @pl.when(pl.program_id(1) == 0)
    def _():
        carry_ref[...] = jnp.zeros_like(carry_ref)

    t = zf_ref[0] + bias_ref[...]
    log_f = jnp.minimum(t, 0.0) - jnp.log(1.0 + jnp.exp(-jnp.abs(t)))
    tc = log_f.shape[0]
    row = lax.broadcasted_iota(I32, (tc, tc), 0)
    col = lax.broadcasted_iota(I32, (tc, tc), 1)
    tri = jnp.where(col <= row, 1.0, 0.0).astype(F32)
    cs = jnp.dot(tri, log_f, precision=lax.Precision.HIGHEST, preferred_element_type=F32) + carry_ref[...]
    carry_ref[...] = cs[tc - 1:tc, :]
    cs2 = cs * LOG2E
    crow_ref[0] = cs2.T[0:8, :]
    for hh in range(GROUP_HEADS):
        crep_ref[0, hh] = jnp.broadcast_to(cs2[:, hh:hh + 1], (tc, LANES))


def _forget_cumsum(zf, bias, tc):
    b, s_len, _ = zf.shape
    return pl.pallas_call(
        _forget_cumsum_kernel,
        out_shape=(jax.ShapeDtypeStruct((b, GROUP_HEADS, s_len, LANES), F32),
                   jax.ShapeDtypeStruct((b, 8, s_len), F32)),
        grid=(b, s_len // tc),
        in_specs=[pl.BlockSpec((1, tc, LANES), lambda bi, si: (bi, si, 0)),
                  pl.BlockSpec((1, LANES), lambda bi, si: (0, 0))],
        out_specs=(pl.BlockSpec((1, GROUP_HEADS, tc, LANES), lambda bi, si: (bi, 0, si, 0)),
                   pl.BlockSpec((1, 8, tc), lambda bi, si: (bi, 0, si))),
        scratch_shapes=[pltpu.VMEM((1, LANES), F32)],
        compiler_params=_params("parallel", "arbitrary"),
        name="forget_cumsum",
    )(zf, bias)


def _dilated_log2_multiplicity(t):
    n_tiles = DIL_MAX_WINDOW // t + 1
    k = jnp.arange(n_tiles)[:, None, None]
    j = jnp.arange(t)[None, :, None]
    i = jnp.arange(t)[None, None, :]
    delta = k * t + i - j
    mult = jnp.zeros(delta.shape, F32)
    for window, dilation in DIL_CONFIGS:
        mult = mult + ((delta >= 0) & (delta <= window) & (delta % dilation == 0)).astype(F32)
    return jnp.where(mult > 0, jnp.log2(jnp.maximum(mult, 1.0)), NEG_INF).astype(F32)


def _attn_kernel(*refs, mixer, t, lam_init):
    q_ref, k_ref, vt_ref = refs[:3]
    o_ref, acc_ref, sa_ref, sb_ref = refs[-4:]
    extra = refs[3:-4]
    qi = pl.program_id(1)
    sep = mixer == "mla"
    nsub = 2 * GROUP_HEADS if mixer == "diff" else GROUP_HEADS
    per_grp = 1 if sep else nsub // 2
    sub_w = LANES // per_grp

    lane = lax.broadcasted_iota(I32, (1, LANES), 1)
    q = q_ref[0]
    qs = []
    for hh in range(nsub):
        grp, pos = hh // per_grp, hh % per_grp
        qg = q[:, grp * LANES:(grp + 1) * LANES]
        if not sep:
            qg = jnp.where((lane >= pos * sub_w) & (lane < (pos + 1) * sub_w), qg, jnp.zeros_like(qg))
        qs.append(qg)

    if mixer == "fox":
        crep_ref, crow_ref = extra
        q_off = pl.multiple_of(qi * t, t)
        cq = [crow_ref[0, hh:hh + 1, pl.ds(q_off, t)] for hh in range(nsub)]
    if mixer == "dil":
        (lm_ref,) = extra

    key_idx = lax.broadcasted_iota(I32, (t, t), 0)
    qry_idx = lax.broadcasted_iota(I32, (t, t), 1)
    causal = key_idx <= qry_idx

    acc_ref[...] = jnp.zeros_like(acc_ref)

    def scores(j, s_ref):
        off = pl.multiple_of(j * t, t)
        k = k_ref[0, pl.ds(off, t), :]
        for hh in range(nsub):
            grp = hh // per_grp
            s_ref[hh] = lax.dot_general(k[:, grp * LANES:(grp + 1) * LANES], qs[hh],
                                        (((1,), (1,)), ((), ())), preferred_element_type=F32)

    def consume(j, s_ref, ms, diag):
        off = pl.multiple_of(j * t, t)
        ss = []
        for hh in range(nsub):
            s = s_ref[hh]
            if mixer == "fox":
                ck = crep_ref[0, hh, pl.ds(off, t), :]
                s = s + (cq[hh] - jnp.concatenate([ck] * (t // LANES), axis=1))
            if mixer == "dil":
                s = s + lm_ref[qi - j]
            elif diag:
                s = jnp.where(causal, s, NEG_INF)
            ss.append(s)
        new_m = [jnp.maximum(ms[hh], jnp.max(ss[hh], axis=0, keepdims=True)) for hh in range(nsub)]
        ps = [jnp.exp2(ss[hh] - new_m[hh]).astype(BF16) for hh in range(nsub)]
        alphas = [jnp.exp2(ms[hh] - new_m[hh]) for hh in range(nsub)]
        pvs = []
        for hh in range(nsub):
            vh = hh * GROUP_HEADS // nsub
            vt = vt_ref[0, vh * HEAD_DIM:(vh + 1) * HEAD_DIM, pl.ds(off, t)]
            vt1 = jnp.concatenate([vt, jnp.ones((ACC_ROWS - HEAD_DIM, t), BF16)], axis=0)
            pvs.append(jnp.dot(vt1, ps[hh], preferred_element_type=F32))
        for hh in range(nsub):
            acc_ref[hh] = acc_ref[hh] * alphas[hh] + pvs[hh]
        return tuple(new_m)

    def pair(i, ms):
        j = lo + 2 * i
        scores(j + 1, sb_ref)
        ms = consume(j, sa_ref, ms, False)
        scores(j + 2, sa_ref)
        return consume(j + 1, sb_ref, ms, False)

    lo = jnp.maximum(qi - DIL_MAX_WINDOW // t, 0) if mixer == "dil" else 0
    n_full = qi - lo
    init = tuple(jnp.full((1, t), NEG_INF, F32) for _ in range(nsub))
    scores(lo, sa_ref)
    ms = lax.fori_loop(0, n_full // 2, pair, init)

    def odd_tail(ms):
        scores(qi, sb_ref)
        ms = consume(qi - 1, sa_ref, ms, False)
        return consume(qi, sb_ref, ms, True)

    def even_tail(ms):
        return consume(qi, sa_ref, ms, True)

    lax.cond(n_full % 2 == 1, odd_tail, even_tail, ms)

    def head_out(hh):
        a = acc_ref[hh]
        return a[:HEAD_DIM] / a[HEAD_DIM:HEAD_DIM + 1]

    if mixer != "diff":
        o_t = jnp.concatenate([head_out(hh) for hh in range(nsub)], axis=0)
        o_ref[0] = o_t.T.astype(o_ref.dtype)
    else:
        lamv_ref, gsub_ref = extra
        lamv = lamv_ref[...]
        e1 = jnp.exp(jnp.sum(lamv[0:1] * lamv[1:2], axis=1, keepdims=True))
        e2 = jnp.exp(jnp.sum(lamv[2:3] * lamv[3:4], axis=1, keepdims=True))
        lam = e1 - e2 + lam_init
        heads = []
        for vh in range(GROUP_HEADS):
            o = head_out(2 * vh) - lam * head_out(2 * vh + 1)
            ms_h = jnp.mean(o * o, axis=0, keepdims=True)
            heads.append(o * lax.rsqrt(ms_h + NORM_EPS))
        o_t = jnp.concatenate(heads, axis=0)
        o_ref[0] = (o_t.T * gsub_ref[...] * (1.0 - lam_init)).astype(o_ref.dtype)


def _attention(zz, vt, mixer_idx, mixer, t, q_col, k_col, extra=(), extra_specs=(), lam_init=0.0):
    b, s_len, _ = zz.shape
    qw = GROUP_HEADS * LANES if mixer == "mla" else GROUP_WIDTH
    qb, kb = q_col // qw, k_col // qw
    nsub = 2 * GROUP_HEADS if mixer == "diff" else GROUP_HEADS
    return pl.pallas_call(
        functools.partial(_attn_kernel, mixer=mixer, t=t, lam_init=lam_init),
        out_shape=jax.ShapeDtypeStruct((b, s_len, GROUP_WIDTH), BF16),
        grid=(b, s_len // t),
        in_specs=[pl.BlockSpec((1, t, qw), lambda bi, qi: (bi, qi, qb)),
                  pl.BlockSpec((1, s_len, qw), lambda bi, qi: (bi, 0, kb)),
                  pl.BlockSpec((1, GROUP_WIDTH, s_len), lambda bi, qi: (bi, mixer_idx, 0)),
                  *extra_specs],
        out_specs=pl.BlockSpec((1, t, GROUP_WIDTH), lambda bi, qi: (bi, qi, 0)),
        scratch_shapes=[pltpu.VMEM((nsub, ACC_ROWS, t), F32), pltpu.VMEM((nsub, t, t), F32),
                        pltpu.VMEM((nsub, t, t), F32)],
        compiler_params=_params("parallel", "arbitrary"),
        name="attn_" + mixer,
    )(zz, zz, vt, *extra)


def _proj_out_kernel(x_ref, ya_ref, yb_ref, yc_ref, yd_ref, w_ref, o_ref):
    acc = x_ref[...]
    for i, y_ref in enumerate((ya_ref, yb_ref, yc_ref, yd_ref)):
        acc = acc + jnp.dot(y_ref[...], w_ref[i * GROUP_WIDTH:(i + 1) * GROUP_WIDTH, :],
                            preferred_element_type=F32)
    o_ref[...] = acc


def _proj_out(x2, ys, w, tm):
    n, d = x2.shape
    yspec = pl.BlockSpec((tm, GROUP_WIDTH), lambda i: (i, 0))
    return pl.pallas_call(
        _proj_out_kernel,
        out_shape=jax.ShapeDtypeStruct((n, d), F32),
        grid=(n // tm,),
        in_specs=[pl.BlockSpec((tm, d), lambda i: (i, 0)), yspec, yspec, yspec, yspec,
                  pl.BlockSpec((4 * GROUP_WIDTH, d), lambda i: (0, 0))],
        out_specs=pl.BlockSpec((tm, d), lambda i: (i, 0)),
        compiler_params=_params("parallel"),
        name="proj_out",
    )(x2, *ys, w)


def _silu(gate):
    return gate / (1.0 + jnp.exp(-gate))


def _ffn_kernel(x_ref, g_ref, wg_ref, wu_ref, wd_ref, o_ref, h_ref, acc_ref):
    f = pl.program_id(1)

    @pl.when(f == 0)
    def _():
        x = x_ref[...]
        h_ref[...] = _rms(x, g_ref[...]).astype(BF16)
        acc_ref[...] = x

    h = h_ref[...]
    gate = jnp.dot(h, wg_ref[...], preferred_element_type=F32)
    up = jnp.dot(h, wu_ref[...], preferred_element_type=F32)
    act = (_silu(gate) * up).astype(BF16)
    acc_ref[...] += jnp.dot(act, wd_ref[...], preferred_element_type=F32)

    @pl.when(f == pl.num_programs(1) - 1)
    def _():
        o_ref[...] = acc_ref[...]


def _ffn(x2, g, wg, wu, wd, tm, tf):
    n, d = x2.shape
    ff = wg.shape[1]
    return pl.pallas_call(
        _ffn_kernel,
        out_shape=jax.ShapeDtypeStruct((n, d), F32),
        grid=(n // tm, ff // tf),
        in_specs=[pl.BlockSpec((tm, d), lambda i, f: (i, 0)),
                  pl.BlockSpec((1, d), lambda i, f: (0, 0)),
                  pl.BlockSpec((d, tf), lambda i, f: (0, f)),
                  pl.BlockSpec((d, tf), lambda i, f: (0, f)),
                  pl.BlockSpec((tf, d), lambda i, f: (f, 0))],
        out_specs=pl.BlockSpec((tm, d), lambda i, f: (i, 0)),
        scratch_shapes=[pltpu.VMEM((tm, d), BF16), pltpu.VMEM((tm, d), F32)],
        compiler_params=_params("parallel", "arbitrary"),
        name="ffn_dense",
    )(x2, g, wg, wu, wd)


def _router_kernel(x_ref, g_ref, wr_ref, h_ref, route_ref, cstart_ref, ctotal_ref, carry_ref):
    @pl.when(pl.program_id(0) == 0)
    def _():
        carry_ref[...] = jnp.zeros_like(carry_ref)

    h = _rms(x_ref[...], g_ref[...])
    h_ref[...] = h.astype(BF16)
    logits = jnp.dot(h, wr_ref[...], precision=lax.Precision.HIGHEST, preferred_element_type=F32)
    lane = lax.broadcasted_iota(I32, logits.shape, 1)
    lg = jnp.where(lane < N_EXPERTS, logits, -jnp.inf)
    m1 = jnp.max(lg, axis=1, keepdims=True)
    i1 = jnp.min(jnp.where(lg == m1, lane, LANES), axis=1, keepdims=True)
    lg2 = jnp.where(lane == i1, -jnp.inf, lg)
    m2 = jnp.max(lg2, axis=1, keepdims=True)
    i2 = jnp.min(jnp.where(lg2 == m2, lane, LANES), axis=1, keepdims=True)
    e = jnp.exp(m2 - m1)
    w1 = 1.0 / (1.0 + e)
    w2 = e / (1.0 + e)
    tm = lg.shape[0]
    chosen = jnp.where(lane == i1, 1.0, jnp.where(lane == i2, 1.0, 0.0))
    row = lax.broadcasted_iota(I32, (tm, tm), 0)
    col = lax.broadcasted_iota(I32, (tm, tm), 1)
    tri = jnp.where(col < row, 1.0, 0.0).astype(BF16)
    carry = carry_ref[...]
    before = jnp.dot(tri, chosen.astype(BF16), preferred_element_type=F32) + carry
    r1 = jnp.sum(jnp.where(lane == i1, before, 0.0), axis=1, keepdims=True)
    r2 = jnp.sum(jnp.where(lane == i2, before, 0.0), axis=1, keepdims=True)
    vals = (i1.astype(F32), i2.astype(F32), w1, w2, r1, r2)
    route = jnp.zeros_like(lg)
    for idx, val in enumerate(vals):
        route = jnp.where(lane == idx, val, route)
    route_ref[...] = route
    after = carry + jnp.sum(chosen, axis=0, keepdims=True)
    cstart_ref[0] = jnp.broadcast_to(carry, cstart_ref.shape[1:])
    ctotal_ref[...] = jnp.broadcast_to(after, ctotal_ref.shape)
    carry_ref[...] = after


def _router(x2, g, w_router, tm):
    n, d = x2.shape
    return pl.pallas_call(
        _router_kernel,
        out_shape=(jax.ShapeDtypeStruct((n, d), BF16), jax.ShapeDtypeStruct((n, LANES), F32),
                   jax.ShapeDtypeStruct((n // tm, 8, LANES), F32), jax.ShapeDtypeStruct((8, LANES), F32)),
        grid=(n // tm,),
        in_specs=[pl.BlockSpec((tm, d), lambda i: (i, 0)),
                  pl.BlockSpec((1, d), lambda i: (0, 0)),
                  pl.BlockSpec((d, LANES), lambda i: (0, 0))],
        out_specs=(pl.BlockSpec((tm, d), lambda i: (i, 0)), pl.BlockSpec((tm, LANES), lambda i: (i, 0)),
                   pl.BlockSpec((1, 8, LANES), lambda i: (i, 0, 0)), pl.BlockSpec((8, LANES), lambda i: (0, 0))),
        scratch_shapes=[pltpu.VMEM((1, LANES), F32)],
        compiler_params=_params("arbitrary"),
        name="moe_router",
    )(x2, g, w_router)


def _dispatch_kernel(vtile_ref, vchunk_ref, vfirst_ref, vlo_ref, vhi_ref, nvis_ref,
                     pos1_ref, pos2_ref, h_ref, o_ref):
    v = pl.program_id(0)

    @pl.when(v < nvis_ref[0])
    def _():
        ts, tc = o_ref.shape[0], h_ref.shape[0]
        first = vfirst_ref[v] == 1
        lo, hi = vlo_ref[v], vhi_ref[v]
        for blk in range(ts // SUB_DISPATCH):
            r0 = blk * SUB_DISPATCH
            rows = slice(r0, r0 + SUB_DISPATCH)
            active = (lo < r0 + SUB_DISPATCH) & (hi > r0)

            def gathered(r0=r0):
                slot = vtile_ref[v] * ts + r0 + lax.broadcasted_iota(I32, (SUB_DISPATCH, tc), 0)
                hit = jnp.where(slot == pos1_ref[...], 1.0, jnp.where(slot == pos2_ref[...], 1.0, 0.0))
                return jnp.dot(hit.astype(BF16), h_ref[...], preferred_element_type=F32).astype(o_ref.dtype)

            @pl.when(active & first)
            def _():
                o_ref[rows, :] = gathered()

            @pl.when(active & jnp.logical_not(first))
            def _():
                o_ref[rows, :] = o_ref[rows, :] + gathered()

            @pl.when(jnp.logical_not(active) & first)
            def _():
                o_ref[rows, :] = jnp.zeros((SUB_DISPATCH, o_ref.shape[1]), o_ref.dtype)


def _dispatch(h, pos1, pos2, visits, n_slots, ts, tc):
    n, d = h.shape
    nvis = visits[-1]
    imap = lambda f: (lambda v, vt, vc, vf, vlo, vhi, nv: f(vt, vc, v))
    grid_spec = pltpu.PrefetchScalarGridSpec(
        num_scalar_prefetch=len(visits),
        grid=(visits[0].shape[0],),
        in_specs=[pl.BlockSpec((1, tc), imap(lambda vt, vc, v: (0, vc[v]))),
                  pl.BlockSpec((1, tc), imap(lambda vt, vc, v: (0, vc[v]))),
                  pl.BlockSpec((tc, d), imap(lambda vt, vc, v: (vc[v], 0)))],
        out_specs=pl.BlockSpec((ts, d), imap(lambda vt, vc, v: (vt[v], 0))),
    )
    return pl.pallas_call(
        _dispatch_kernel,
        out_shape=jax.ShapeDtypeStruct((n_slots, d), BF16),
        grid_spec=grid_spec,
        compiler_params=_params("arbitrary"),
        name="moe_dispatch",
    )(*visits, pos1, pos2, h)


def _expert_ffn_kernel(texp_ref, nused_ref, xs_ref, wg_ref, wu_ref, wd_ref, o_ref, acc_ref):
    i = pl.program_id(0)
    f = pl.program_id(1)

    @pl.when(f == 0)
    def _():
        acc_ref[...] = jnp.zeros_like(acc_ref)

    @pl.when(i < nused_ref[0])
    def _():
        xs = xs_ref[...]
        gate = jnp.dot(xs, wg_ref[0], preferred_element_type=F32)
        up = jnp.dot(xs, wu_ref[0], preferred_element_type=F32)
        act = (_silu(gate) * up).astype(BF16)
        acc_ref[...] += jnp.dot(act, wd_ref[0], preferred_element_type=F32)

    @pl.when(f == pl.num_programs(1) - 1)
    def _():
        o_ref[...] = acc_ref[...].astype(o_ref.dtype)


def _expert_ffn(xs, tile_expert, n_used, wg, wu, wd, ts, tf):
    n_slots, d = xs.shape
    ff = wg.shape[2]

    def fchunk(i, f, nu):
        return jnp.where(i < nu[0], f, 0)

    grid_spec = pltpu.PrefetchScalarGridSpec(
        num_scalar_prefetch=2,
        grid=(n_slots // ts, ff // tf),
        in_specs=[pl.BlockSpec((ts, d), lambda i, f, te, nu: (i, 0)),
                  pl.BlockSpec((1, d, tf), lambda i, f, te, nu: (te[i], 0, fchunk(i, f, nu))),
                  pl.BlockSpec((1, d, tf), lambda i, f, te, nu: (te[i], 0, fchunk(i, f, nu))),
                  pl.BlockSpec((1, tf, d), lambda i, f, te, nu: (te[i], fchunk(i, f, nu), 0))],
        out_specs=pl.BlockSpec((ts, d), lambda i, f, te, nu: (i, 0)),
        scratch_shapes=[pltpu.VMEM((ts, d), F32)],
    )
    return pl.pallas_call(
        _expert_ffn_kernel,
        out_shape=jax.ShapeDtypeStruct((n_slots, d), BF16),
        grid_spec=grid_spec,
        compiler_params=_params("arbitrary", "arbitrary"),
        name="moe_expert_ffn",
    )(tile_expert, n_used, xs, wg, wu, wd)


def _combine_kernel(vchunk_ref, vtile_ref, vfirst_ref, vlast_ref, vlo_ref, vhi_ref, nvis_ref,
                    route_ref, gstart_ref, ys_ref, x_ref, gf_ref, o_ref, acc_ref, cols_ref):
    v = pl.program_id(0)

    @pl.when(v < nvis_ref[0])
    def _():
        tc, ts = x_ref.shape[0], ys_ref.shape[0]

        @pl.when(vfirst_ref[v] == 1)
        def _():
            acc_ref[...] = x_ref[...]
            route = route_ref[...]
            lane = lax.broadcasted_iota(I32, (1, LANES), 1).astype(F32)
            for which in range(TOP_K):
                start = jnp.sum(jnp.where(lane == route[:, which:which + 1], gstart_ref[...], 0.0),
                                axis=1, keepdims=True)
                pos = start + route[:, 4 + which:5 + which]
                cols_ref[which] = jnp.broadcast_to(pos, (tc, LANES))
                cols_ref[TOP_K + which] = jnp.broadcast_to(route[:, 2 + which:3 + which], (tc, LANES))

        rep = SUB_COMBINE // LANES
        wide = lambda idx: jnp.concatenate([cols_ref[idx]] * rep, axis=1)
        lo, hi = vlo_ref[v], vhi_ref[v]
        for blk in range(ts // SUB_COMBINE):
            c0 = blk * SUB_COMBINE

            @pl.when((lo < c0 + SUB_COMBINE) & (hi > c0))
            def _(c0=c0):
                slot = (vtile_ref[v] * ts + c0 + lax.broadcasted_iota(I32, (1, SUB_COMBINE), 1)).astype(F32)
                gmat = jnp.where(slot == wide(0), wide(2), jnp.where(slot == wide(1), wide(3), 0.0))
                acc_ref[...] += jnp.dot(gmat.astype(BF16), ys_ref[c0:c0 + SUB_COMBINE, :],
                                        preferred_element_type=F32)

        @pl.when(vlast_ref[v] == 1)
        def _():
            o_ref[...] = _rms(acc_ref[...], gf_ref[...])


def _combine(x2, ys, route, gstart_row, g_final, visits, ts, tc):
    n, d = x2.shape
    imap = lambda f: (lambda v, vc, vt, vf, vl, vlo, vhi, nv: f(vc, vt, v))
    grid_spec = pltpu.PrefetchScalarGridSpec(
        num_scalar_prefetch=len(visits),
        grid=(visits[0].shape[0],),
        in_specs=[pl.BlockSpec((tc, LANES), imap(lambda vc, vt, v: (vc[v], 0))),
                  pl.BlockSpec((1, LANES), imap(lambda vc, vt, v: (0, 0))),
                  pl.BlockSpec((ts, d), imap(lambda vc, vt, v: (vt[v], 0))),
                  pl.BlockSpec((tc, d), imap(lambda vc, vt, v: (vc[v], 0))),
                  pl.BlockSpec((1, d), imap(lambda vc, vt, v: (0, 0)))],
        out_specs=pl.BlockSpec((tc, d), imap(lambda vc, vt, v: (vc[v], 0))),
        scratch_shapes=[pltpu.VMEM((tc, d), F32), pltpu.VMEM((2 * TOP_K, tc, LANES), F32)],
    )
    return pl.pallas_call(
        _combine_kernel,
        out_shape=jax.ShapeDtypeStruct((n, d), F32),
        grid_spec=grid_spec,
        compiler_params=_params("arbitrary"),
        name="moe_combine",
    )(*visits, route, gstart_row, ys, x2, g_final)


def _count_le(ends, v):
    return jnp.sum((ends[None, :] <= v[:, None]).astype(I32), axis=1)


def _flatten_visits(counts, n_max):
    ends = jnp.cumsum(counts)
    total = ends[-1]
    v = jnp.minimum(jnp.arange(n_max, dtype=I32), total - 1)
    seg = _count_le(ends, v)
    within = v - (ends[seg] - counts[seg])
    return seg, within, (within == 0).astype(I32), total.astype(I32)


def _moe_top2(x2, g, w_router, wg, wu, wd, g_final):
    n, d = x2.shape
    ts, tc = T_SLOT, min(T_CHUNK, n)
    n_chunks = n // tc
    n_tiles = TOP_K * n // ts + N_EXPERTS
    n_slots = n_tiles * ts
    n_visits = n_tiles + N_EXPERTS * n_chunks

    h, route, cstart, ctotal = _router(x2, g, w_router, tc)

    experts = jnp.arange(N_EXPERTS, dtype=I32)
    counts = ctotal[0, :N_EXPERTS].astype(I32)
    padded = (counts + ts - 1) // ts * ts
    group_end = jnp.cumsum(padded)
    group_start = group_end - padded
    n_used = (group_end[-1] // ts).astype(I32)
    tiles = jnp.arange(n_tiles, dtype=I32)
    tile_expert = jnp.minimum(_count_le(group_end // ts, tiles), N_EXPERTS - 1)
    c_before = cstart[:, 0, :N_EXPERTS].astype(I32)
    c_after = jnp.concatenate([c_before[1:], counts[None, :]], axis=0)

    e1, e2 = route[:, 0].astype(I32), route[:, 1].astype(I32)
    start_of = lambda e: jnp.sum(jnp.where(e[:, None] == experts, group_start, 0), axis=1)
    pos1 = start_of(e1) + route[:, 4].astype(I32)
    pos2 = start_of(e2) + route[:, 5].astype(I32)

    rank_lo = tiles * ts - group_start[tile_expert]
    rank_hi = jnp.minimum(rank_lo + ts, counts[tile_expert]) - 1
    after_t = c_after[:, tile_expert]
    used = tiles < n_used
    c_lo = jnp.where(used, jnp.sum((after_t <= rank_lo).astype(I32), axis=0), 0)
    c_hi = jnp.where(used, jnp.sum((after_t <= rank_hi).astype(I32), axis=0), 0)
    d_tile, d_within, d_first, d_total = _flatten_visits(c_hi - c_lo + 1, n_visits)
    d_chunk = c_lo[d_tile] + d_within

    def slot_range(chunk, expert, tile):
        lo = group_start[expert] + c_before[chunk, expert] - tile * ts
        hi = group_start[expert] + c_after[chunk, expert] - tile * ts
        return jnp.clip(lo, 0, ts), jnp.clip(hi, 0, ts)

    d_lo, d_hi = slot_range(d_chunk, tile_expert[d_tile], d_tile)
    d_real = (d_tile < n_used).astype(I32)
    d_visits = (d_tile, d_chunk, d_first, d_lo * d_real, d_hi * d_real, d_total.reshape(1))

    t_lo = (group_start + c_before) // ts
    t_hi = (group_start + c_after - 1) // ts
    seg_counts = jnp.where(c_after > c_before, t_hi - t_lo + 1, 0).reshape(-1)
    c_seg, c_within, _, c_total = _flatten_visits(seg_counts, n_visits)
    c_chunk = c_seg // N_EXPERTS
    c_tile = t_lo.reshape(-1)[c_seg] + c_within
    prev_chunk = jnp.concatenate([jnp.full((1,), -1, I32), c_chunk[:-1]])
    next_chunk = jnp.concatenate([c_chunk[1:], jnp.full((1,), -1, I32)])
    vid = jnp.arange(n_visits, dtype=I32)
    c_first = (c_chunk != prev_chunk).astype(I32)
    c_last = ((c_chunk != next_chunk) | (vid == c_total - 1)).astype(I32)
    v_lo, v_hi = slot_range(c_chunk, c_seg % N_EXPERTS, c_tile)
    c_visits = (c_chunk, c_tile, c_first, c_last, v_lo, v_hi, c_total.reshape(1))

    xs = _dispatch(h, pos1.reshape(1, n), pos2.reshape(1, n), d_visits, n_slots, ts, tc)
    ys = _expert_ffn(xs, tile_expert, n_used.reshape(1), wg, wu, wd, ts, TF_EXPERT)
    gstart_row = _pad_cols(group_start.astype(F32).reshape(1, N_EXPERTS), LANES)
    return _combine(x2, ys, route, gstart_row, g_final, c_visits, ts, tc)


def _pad_cols(w, width):
    return jnp.pad(w, ((0, 0), (0, width - w.shape[1])))


def _prep_w_in(w):
    parts = []
    off = 0
    for size in IN_SIZES:
        parts.append(w[:, off:off + size])
        off += size
    cq, ckv, kr, q_b, k_b, v_b, q_c, k_c, v_c, f_c, q_d, k_d, v_d = parts
    d = w.shape[0]
    kr_blk = jnp.concatenate([jnp.zeros((d, MLA_NOPE), w.dtype), kr,
                              jnp.zeros((d, LANES - MLA_NOPE - MLA_ROPE), w.dtype)], axis=1)
    cols = [cq, ckv, kr_blk, _pad_cols(f_c, LANES), q_b, k_b, v_b, q_c, k_c, v_c, q_d, k_d, v_d]
    return jnp.concatenate(cols, axis=1).astype(BF16)


def _prep_mla(w_uq, w_ukv):
    qk = MLA_NOPE + MLA_ROPE
    wuq = jnp.pad(w_uq.reshape(MLA_Q_RANK, GROUP_HEADS, qk), ((0, 0), (0, 0), (0, LANES - qk)))
    wuq = wuq.reshape(MLA_Q_RANK, GROUP_HEADS * LANES)
    kv = w_ukv.reshape(MLA_KV_RANK, GROUP_HEADS, MLA_NOPE + HEAD_DIM)
    wuk = jnp.pad(kv[:, :, :MLA_NOPE], ((0, 0), (0, 0), (0, LANES - MLA_NOPE)))
    wuk = wuk.reshape(MLA_KV_RANK, GROUP_HEADS * LANES)
    wuv = kv[:, :, MLA_NOPE:].reshape(MLA_KV_RANK, GROUP_WIDTH)
    return wuq.astype(BF16), wuk.astype(BF16), wuv.astype(BF16)


def _row(v, width=None):
    v = v.reshape(1, -1).astype(F32)
    return v if width is None else _pad_cols(v, width)


def kernel(x, g_mix, w_in, b_forget, g_q_lat, g_kv_lat, w_uq, w_ukv, lambda_q1, lambda_k1, lambda_q2,
           lambda_k2, g_diff_sub, w_out, g_ffn, w_ffn_gate, w_ffn_up, w_ffn_down, w_router, w_exp_gate,
           w_exp_up, w_exp_down, g_final):
    b, s_len, d = x.shape
    depth = g_mix.shape[0]
    assert d == D_MODEL and depth == 2, "kernel is specialised to the two-layer trunk"
    assert w_router.shape[-1] == N_EXPERTS
    n = b * s_len
    t_attn = T_ATTN
    tm_in = min(TM_PROJ, s_len)
    tm_tok = min(TM_TOK, n)
    assert s_len % t_attn == 0 and s_len % tm_in == 0 and n % tm_tok == 0
    assert n % min(T_CHUNK, n) == 0 and (TOP_K * n) % T_SLOT == 0

    tabs = _rope_tables(s_len)
    log_mult = _dilated_log2_multiplicity(t_attn)
    n_lm = log_mult.shape[0]

    for l in range(depth):
        wuq, wuk, wuv = _prep_mla(w_uq[l], w_ukv[l])
        zz, vt, zf = _proj_in(x, _row(g_mix[l]), _prep_w_in(w_in[l]), _row(g_q_lat[l]),
                              _row(g_kv_lat[l]), wuq, wuk, wuv, tabs, tm_in)
        crep, crow = _forget_cumsum(zf, _row(b_forget[l], LANES), tm_in)

        lam_init = 0.8 - 0.6 * math.exp(-0.3 * l)
        lamv = jnp.concatenate(
            [_row(v, LANES) for v in (lambda_q1[l], lambda_k1[l], lambda_q2[l], lambda_k2[l])]
            + [jnp.zeros((4, LANES), F32)], axis=0)
        gsub = jnp.tile(_row(g_diff_sub[l]), (1, GROUP_HEADS))

        y_a = _attention(zz, vt, 0, "mla", t_attn, ZZ_QA, ZZ_KA)
        c0 = ZZ_QB
        y_b = _attention(zz, vt, 1, "dil", t_attn, c0, c0 + GROUP_WIDTH, extra=(log_mult,),
                         extra_specs=(pl.BlockSpec((n_lm, t_attn, t_attn), lambda bi, qi: (0, 0, 0)),))
        c0 += 2 * GROUP_WIDTH
        y_c = _attention(zz, vt, 2, "fox", t_attn, c0, c0 + GROUP_WIDTH, extra=(crep, crow),
                         extra_specs=(pl.BlockSpec((1, GROUP_HEADS, s_len, LANES), lambda bi, qi: (bi, 0, 0, 0)),
                                      pl.BlockSpec((1, 8, s_len), lambda bi, qi: (bi, 0, 0))))
        c0 += 2 * GROUP_WIDTH
        y_d = _attention(zz, vt, 3, "diff", t_attn, c0, c0 + GROUP_WIDTH, extra=(lamv, gsub),
                         extra_specs=(pl.BlockSpec((8, LANES), lambda bi, qi: (0, 0)),
                                      pl.BlockSpec((1, GROUP_WIDTH), lambda bi, qi: (0, 0))),
                         lam_init=lam_init)

        ys = [y.reshape(n, GROUP_WIDTH) for y in (y_a, y_b, y_c, y_d)]
        x2 = _proj_out(x.reshape(n, d), ys, w_out[l].astype(BF16), tm_tok)

        i = l // 2
        if l % 2 == 0:
            x2 = _ffn(x2, _row(g_ffn[l]), w_ffn_gate[i].astype(BF16), w_ffn_up[i].astype(BF16),
                      w_ffn_down[i].astype(BF16), tm_tok, TF_DENSE)
        else:
            x2 = _moe_top2(x2, _row(g_ffn[l]), _pad_cols(w_router[i].astype(F32), LANES),
                           w_exp_gate[i].astype(BF16), w_exp_up[i].astype(BF16),
                           w_exp_down[i].astype(BF16), _row(g_final))
        x = x2.reshape(b, s_len, d)
    return x
```

```python
import functools
import math

import jax
import jax.numpy as jnp
from jax import lax
from jax.experimental import pallas as pl
from jax.experimental.pallas import tpu as pltpu

F32 = jnp.float32
BF16 = jnp.bfloat16
I32 = jnp.int32

D_MODEL = 1024
HEAD_DIM = 64
GROUP_HEADS = 4
GROUP_WIDTH = GROUP_HEADS * HEAD_DIM
N_MIXERS = 4
MLA_Q_RANK = 256
MLA_KV_RANK = 128
MLA_NOPE = 64
MLA_ROPE = 32
DIL_CONFIGS = ((128, 1), (512, 4), (2048, 16))
DIL_MAX_WINDOW = max(w for w, _ in DIL_CONFIGS)
DIFF_QK = HEAD_DIM // 2
N_EXPERTS = 8
TOP_K = 2
ROPE_THETA = 10000.0
NORM_EPS = 1e-6
NEG_INF = -1e30
LOG2E = math.log2(math.e)
ACC_ROWS = HEAD_DIM + 16
LANES = 128
IN_SIZES = (MLA_Q_RANK, MLA_KV_RANK, MLA_ROPE,
            GROUP_WIDTH, GROUP_WIDTH, GROUP_WIDTH,
            GROUP_WIDTH, GROUP_WIDTH, GROUP_WIDTH, GROUP_HEADS,
            GROUP_WIDTH, GROUP_WIDTH, GROUP_WIDTH)

WP_CQ, WP_CKV, WP_KR, WP_F = 0, 256, 384, 512
WP_QB = 640
WP_WIDTH = WP_QB + 9 * GROUP_WIDTH

ZZ_QA, ZZ_KA = 0, 512
ZZ_QB = 1024
ZZ_WIDTH = ZZ_QB + 6 * GROUP_WIDTH

ROPE_A, ROPE_B, ROPE_D = 0, 1, 2

VMEM_LIMIT = 56 * 1024 * 1024

T_ATTN = 256
TM_PROJ = 512
TM_TOK = 1024
TF_DENSE = 256
T_SLOT = 512
T_CHUNK = 512
TF_EXPERT = 1792
SUB_DISPATCH = 128
ROW_ALIGN = 16
COMBINE_ROWS = 256
COMBINE_MAX_BLOCKS = -(-(T_CHUNK + ROW_ALIGN - 1) // COMBINE_ROWS)


def _params(*sem):
    return pltpu.CompilerParams(dimension_semantics=sem, vmem_limit_bytes=VMEM_LIMIT)


def _rope_tables(s_len):
    pos = jnp.arange(s_len, dtype=F32)[:, None]
    lane = jnp.arange(LANES)[None, :]

    def kind(rel, width, active):
        half = width // 2
        idx = (rel % half).astype(F32)
        inv = ROPE_THETA ** (-idx / half)
        ang = pos * inv
        cos, sin = jnp.cos(ang), jnp.sin(ang)
        first = active & (rel < half)
        second = active & (rel >= half)
        c = jnp.where(active, cos, 1.0)
        sm = jnp.where(first, -sin, 0.0)
        sp = jnp.where(second, sin, 0.0)
        return [c, sm, sp]

    a_active = (lane >= MLA_NOPE) & (lane < MLA_NOPE + MLA_ROPE)
    tabs = (kind(jnp.where(a_active, lane - MLA_NOPE, 0), MLA_ROPE, a_active)
            + kind(lane % HEAD_DIM, HEAD_DIM, lane >= 0)
            + kind(lane % DIFF_QK, DIFF_QK, lane >= 0))
    return jnp.stack([jnp.broadcast_to(t, (s_len, LANES)).astype(F32) for t in tabs])


_ROPE_HALF = {ROPE_A: MLA_ROPE // 2, ROPE_B: HEAD_DIM // 2, ROPE_D: DIFF_QK // 2}


def _rope(blk, tab_ref, kind):
    half = _ROPE_HALF[kind]
    c = tab_ref[3 * kind]
    sm = tab_ref[3 * kind + 1]
    sp = tab_ref[3 * kind + 2]
    return blk * c + pltpu.roll(blk, LANES - half, 1) * sm + pltpu.roll(blk, half, 1) * sp


def _rms(x, g):
    return x * lax.rsqrt(jnp.mean(x * x, axis=-1, keepdims=True) + NORM_EPS) * g


def _proj_in_kernel(x_ref, g_ref, w_ref, gql_ref, gkvl_ref, wuq_ref, wuk_ref, wuv_ref, tab_ref,
                    zz_ref, vt_ref, zf_ref):
    h = _rms(x_ref[0], g_ref[...]).astype(BF16)

    def proj(c0, width):
        return jnp.dot(h, w_ref[:, c0:c0 + width], preferred_element_type=F32)

    def put(col, val):
        zz_ref[0, :, col:col + LANES] = val.astype(BF16)

    def put_v(mixer, val):
        vt_ref[0, mixer * GROUP_WIDTH:(mixer + 1) * GROUP_WIDTH, :] = val.T.astype(BF16)

    latent = {}
    scale_a = (MLA_NOPE + MLA_ROPE) ** -0.5 * LOG2E
    plan = (((ROPE_B, HEAD_DIM ** -0.5 * LOG2E), (ROPE_B, 1.0)),
            ((None, HEAD_DIM ** -0.5 * LOG2E), (None, 1.0)),
            ((ROPE_D, DIFF_QK ** -0.5 * LOG2E), (ROPE_D, 1.0)))

    def group(idx):
        return lambda: proj(WP_QB + idx * GROUP_WIDTH, GROUP_WIDTH)

    def qk_epilogue(mx, t, kind, scale, z):
        for half in range(GROUP_WIDTH // LANES):
            blk = z[:, half * LANES:(half + 1) * LANES]
            if kind is not None:
                blk = _rope(blk, tab_ref, kind)
            if scale != 1.0:
                blk = blk * scale
            put(ZZ_QB + (2 * mx + t) * GROUP_WIDTH + half * LANES, blk)

    def mla_epilogue(col, scale, z):
        for hh in range(GROUP_HEADS):
            blk = _rope(z[:, hh * LANES:(hh + 1) * LANES], tab_ref, ROPE_A)
            put(col + hh * LANES, blk * scale if scale != 1.0 else blk)

    def keep(name, g_ref_, z):
        latent[name] = _rms(z, g_ref_[...]).astype(BF16)

    def store_f(z):
        zf_ref[0] = z

    def mixer_tasks(mx):
        out = [(group(3 * mx + t), functools.partial(qk_epilogue, mx, t, kind, scale))
               for t, (kind, scale) in enumerate(plan[mx])]
        return out + [(group(3 * mx + 2), functools.partial(put_v, mx + 1))]

    tasks = [
        (lambda: proj(WP_CQ, MLA_Q_RANK), functools.partial(keep, "cq", gql_ref)),
        (lambda: proj(WP_CKV, MLA_KV_RANK), functools.partial(keep, "ckv", gkvl_ref)),
        *mixer_tasks(0)[:2],
        (lambda: jnp.dot(latent["cq"], wuq_ref[...], preferred_element_type=F32),
         functools.partial(mla_epilogue, ZZ_QA, scale_a)),
        (lambda: jnp.dot(latent["ckv"], wuk_ref[...], preferred_element_type=F32)
         + jnp.concatenate([proj(WP_KR, LANES)] * GROUP_HEADS, axis=1),
         functools.partial(mla_epilogue, ZZ_KA, 1.0)),
        (lambda: jnp.dot(latent["ckv"], wuv_ref[...], preferred_element_type=F32), functools.partial(put_v, 0)),
        (lambda: proj(WP_F, LANES), store_f),
        *mixer_tasks(0)[2:], *mixer_tasks(1), *mixer_tasks(2),
    ]
    z = tasks[0][0]()
    for i, (_, epilogue) in enumerate(tasks):
        z_next = tasks[i + 1][0]() if i + 1 < len(tasks) else None
        epilogue(z)
        z = z_next


def _proj_in(x, g, w, gql, gkvl, wuq, wuk, wuv, tabs, tm):
    b, s_len, d = x.shape
    const = lambda shape: pl.BlockSpec(shape, lambda bi, si: (0,) * len(shape))
    return pl.pallas_call(
        _proj_in_kernel,
        out_shape=(jax.ShapeDtypeStruct((b, s_len, ZZ_WIDTH), BF16),
                   jax.ShapeDtypeStruct((b, N_MIXERS * GROUP_WIDTH, s_len), BF16),
                   jax.ShapeDtypeStruct((b, s_len, LANES), F32)),
        grid=(b, s_len // tm),
        in_specs=[
            pl.BlockSpec((1, tm, d), lambda bi, si: (bi, si, 0)),
            const((1, d)), const((d, WP_WIDTH)),
            const((1, MLA_Q_RANK)), const((1, MLA_KV_RANK)),
            const((MLA_Q_RANK, GROUP_HEADS * LANES)),
            const((MLA_KV_RANK, GROUP_HEADS * LANES)),
            const((MLA_KV_RANK, GROUP_WIDTH)),
            pl.BlockSpec((9, tm, LANES), lambda bi, si: (0, si, 0)),
        ],
        out_specs=(pl.BlockSpec((1, tm, ZZ_WIDTH), lambda bi, si: (bi, si, 0)),
                   pl.BlockSpec((1, N_MIXERS * GROUP_WIDTH, tm), lambda bi, si: (bi, 0, si)),
                   pl.BlockSpec((1, tm, LANES), lambda bi, si: (bi, si, 0))),
        compiler_params=_params("parallel", "parallel"),
        name="proj_in",
    )(x, g, w, gql, gkvl, wuq, wuk, wuv, tabs)


def _forget_cumsum_kernel(zf_ref, bias_ref, crep_ref, crow_ref, carry_ref):
    @pl.when(pl.program_id(1) == 0)
    def _():
        carry_ref[...] = jnp.zeros_like(carry_ref)

    t = zf_ref[0] + bias_ref[...]
    log_f = jnp.minimum(t, 0.0) - jnp.log(1.0 + jnp.exp(-jnp.abs(t)))
    tc = log_f.shape[0]
    row = lax.broadcasted_iota(I32, (tc, tc), 0)
    col = lax.broadcasted_iota(I32, (tc, tc), 1)
    tri = jnp.where(col <= row, 1.0, 0.0).astype(F32)
    cs = jnp.dot(tri, log_f, precision=lax.Precision.HIGHEST, preferred_element_type=F32) + carry_ref[...]
    carry_ref[...] = cs[tc - 1:tc, :]
    cs2 = cs * LOG2E
    crow_ref[0] = cs2.T[0:8, :]
    for hh in range(GROUP_HEADS):
        crep_ref[0, hh] = jnp.broadcast_to(cs2[:, hh:hh + 1], (tc, LANES))


def _forget_cumsum(zf, bias, tc):
    b, s_len, _ = zf.shape
    return pl.pallas_call(
        _forget_cumsum_kernel,
        out_shape=(jax.ShapeDtypeStruct((b, GROUP_HEADS, s_len, LANES), F32),
                   jax.ShapeDtypeStruct((b, 8, s_len), F32)),
        grid=(b, s_len // tc),
        in_specs=[pl.BlockSpec((1, tc, LANES), lambda bi, si: (bi, si, 0)),
                  pl.BlockSpec((1, LANES), lambda bi, si: (0, 0))],
        out_specs=(pl.BlockSpec((1, GROUP_HEADS, tc, LANES), lambda bi, si: (bi, 0, si, 0)),
                   pl.BlockSpec((1, 8, tc), lambda bi, si: (bi, 0, si))),
        scratch_shapes=[pltpu.VMEM((1, LANES), F32)],
        compiler_params=_params("parallel", "arbitrary"),
        name="forget_cumsum",
    )(zf, bias)


def _dilated_log2_multiplicity(t):
    n_tiles = DIL_MAX_WINDOW // t + 1
    k = jnp.arange(n_tiles)[:, None, None]
    j = jnp.arange(t)[None, :, None]
    i = jnp.arange(t)[None, None, :]
    delta = k * t + i - j
    mult = jnp.zeros(delta.shape, F32)
    for window, dilation in DIL_CONFIGS:
        mult = mult + ((delta >= 0) & (delta <= window) & (delta % dilation == 0)).astype(F32)
    return jnp.where(mult > 0, jnp.log2(jnp.maximum(mult, 1.0)), NEG_INF).astype(F32)


def _attn_kernel(*refs, mixer, t, lam_init):
    q_ref, k_ref, vt_ref = refs[:3]
    o_ref, acc_ref, sa_ref, sb_ref = refs[-4:]
    extra = refs[3:-4]
    qi = pl.program_id(1)
    sep = mixer == "mla"
    nsub = 2 * GROUP_HEADS if mixer == "diff" else GROUP_HEADS
    per_grp = 1 if sep else nsub // 2
    sub_w = LANES // per_grp

    lane = lax.broadcasted_iota(I32, (1, LANES), 1)
    q = q_ref[0]
    qs = []
    for hh in range(nsub):
        grp, pos = hh // per_grp, hh % per_grp
        qg = q[:, grp * LANES:(grp + 1) * LANES]
        if not sep:
            qg = jnp.where((lane >= pos * sub_w) & (lane < (pos + 1) * sub_w), qg, jnp.zeros_like(qg))
        qs.append(qg)

    if mixer == "fox":
        crep_ref, crow_ref = extra
        q_off = pl.multiple_of(qi * t, t)
        cq = [crow_ref[0, hh:hh + 1, pl.ds(q_off, t)] for hh in range(nsub)]
    if mixer == "dil":
        (lm_ref,) = extra

    key_idx = lax.broadcasted_iota(I32, (t, t), 0)
    qry_idx = lax.broadcasted_iota(I32, (t, t), 1)
    causal = key_idx <= qry_idx

    acc_ref[...] = jnp.zeros_like(acc_ref)

    def scores(j, s_ref):
        off = pl.multiple_of(j * t, t)
        k = k_ref[0, pl.ds(off, t), :]
        for hh in range(nsub):
            grp = hh // per_grp
            s_ref[hh] = lax.dot_general(k[:, grp * LANES:(grp + 1) * LANES], qs[hh],
                                        (((1,), (1,)), ((), ())), preferred_element_type=F32)

    def consume(j, s_ref, ms, diag):
        off = pl.multiple_of(j * t, t)
        ss = []
        for hh in range(nsub):
            s = s_ref[hh]
            if mixer == "fox":
                ck = crep_ref[0, hh, pl.ds(off, t), :]
                s = s + (cq[hh] - jnp.concatenate([ck] * (t // LANES), axis=1))
            if mixer == "dil":
                s = s + lm_ref[qi - j]
            elif diag:
                s = jnp.where(causal, s, NEG_INF)
            ss.append(s)
        new_m = [jnp.maximum(ms[hh], jnp.max(ss[hh], axis=0, keepdims=True)) for hh in range(nsub)]
        ps = [jnp.exp2(ss[hh] - new_m[hh]).astype(BF16) for hh in range(nsub)]
        alphas = [jnp.exp2(ms[hh] - new_m[hh]) for hh in range(nsub)]
        pvs = []
        for hh in range(nsub):
            vh = hh * GROUP_HEADS // nsub
            vt = vt_ref[0, vh * HEAD_DIM:(vh + 1) * HEAD_DIM, pl.ds(off, t)]
            vt1 = jnp.concatenate([vt, jnp.ones((ACC_ROWS - HEAD_DIM, t), BF16)], axis=0)
            pvs.append(jnp.dot(vt1, ps[hh], preferred_element_type=F32))
        for hh in range(nsub):
            acc_ref[hh] = acc_ref[hh] * alphas[hh] + pvs[hh]
        return tuple(new_m)

    def pair(i, ms):
        j = lo + 2 * i
        scores(j + 1, sb_ref)
        ms = consume(j, sa_ref, ms, False)
        scores(j + 2, sa_ref)
        return consume(j + 1, sb_ref, ms, False)

    lo = jnp.maximum(qi - DIL_MAX_WINDOW // t, 0) if mixer == "dil" else 0
    n_full = qi - lo
    init = tuple(jnp.full((1, t), NEG_INF, F32) for _ in range(nsub))
    scores(lo, sa_ref)
    ms = lax.fori_loop(0, n_full // 2, pair, init)

    def odd_tail(ms):
        scores(qi, sb_ref)
        ms = consume(qi - 1, sa_ref, ms, False)
        return consume(qi, sb_ref, ms, True)

    def even_tail(ms):
        return consume(qi, sa_ref, ms, True)

    lax.cond(n_full % 2 == 1, odd_tail, even_tail, ms)

    def head_out(hh):
        a = acc_ref[hh]
        return a[:HEAD_DIM] / a[HEAD_DIM:HEAD_DIM + 1]

    if mixer != "diff":
        o_t = jnp.concatenate([head_out(hh) for hh in range(nsub)], axis=0)
        o_ref[0] = o_t.T.astype(o_ref.dtype)
    else:
        lamv_ref, gsub_ref = extra
        lamv = lamv_ref[...]
        e1 = jnp.exp(jnp.sum(lamv[0:1] * lamv[1:2], axis=1, keepdims=True))
        e2 = jnp.exp(jnp.sum(lamv[2:3] * lamv[3:4], axis=1, keepdims=True))
        lam = e1 - e2 + lam_init
        heads = []
        for vh in range(GROUP_HEADS):
            o = head_out(2 * vh) - lam * head_out(2 * vh + 1)
            ms_h = jnp.mean(o * o, axis=0, keepdims=True)
            heads.append(o * lax.rsqrt(ms_h + NORM_EPS))
        o_t = jnp.concatenate(heads, axis=0)
        o_ref[0] = (o_t.T * gsub_ref[...] * (1.0 - lam_init)).astype(o_ref.dtype)


def _attention(zz, vt, mixer_idx, mixer, t, q_col, k_col, extra=(), extra_specs=(), lam_init=0.0):
    b, s_len, _ = zz.shape
    qw = GROUP_HEADS * LANES if mixer == "mla" else GROUP_WIDTH
    qb, kb = q_col // qw, k_col // qw
    nsub = 2 * GROUP_HEADS if mixer == "diff" else GROUP_HEADS
    return pl.pallas_call(
        functools.partial(_attn_kernel, mixer=mixer, t=t, lam_init=lam_init),
        out_shape=jax.ShapeDtypeStruct((b, s_len, GROUP_WIDTH), BF16),
        grid=(b, s_len // t),
        in_specs=[pl.BlockSpec((1, t, qw), lambda bi, qi: (bi, qi, qb)),
                  pl.BlockSpec((1, s_len, qw), lambda bi, qi: (bi, 0, kb)),
                  pl.BlockSpec((1, GROUP_WIDTH, s_len), lambda bi, qi: (bi, mixer_idx, 0)),
                  *extra_specs],
        out_specs=pl.BlockSpec((1, t, GROUP_WIDTH), lambda bi, qi: (bi, qi, 0)),
        scratch_shapes=[pltpu.VMEM((nsub, ACC_ROWS, t), F32), pltpu.VMEM((nsub, t, t), F32),
                        pltpu.VMEM((nsub, t, t), F32)],
        compiler_params=_params("parallel", "arbitrary"),
        name="attn_" + mixer,
    )(zz, zz, vt, *extra)


def _proj_out_kernel(x_ref, ya_ref, yb_ref, yc_ref, yd_ref, w_ref, o_ref):
    acc = x_ref[...]
    for i, y_ref in enumerate((ya_ref, yb_ref, yc_ref, yd_ref)):
        acc = acc + jnp.dot(y_ref[...], w_ref[i * GROUP_WIDTH:(i + 1) * GROUP_WIDTH, :],
                            preferred_element_type=F32)
    o_ref[...] = acc


def _proj_out(x2, ys, w, tm):
    n, d = x2.shape
    yspec = pl.BlockSpec((tm, GROUP_WIDTH), lambda i: (i, 0))
    return pl.pallas_call(
        _proj_out_kernel,
        out_shape=jax.ShapeDtypeStruct((n, d), F32),
        grid=(n // tm,),
        in_specs=[pl.BlockSpec((tm, d), lambda i: (i, 0)), yspec, yspec, yspec, yspec,
                  pl.BlockSpec((4 * GROUP_WIDTH, d), lambda i: (0, 0))],
        out_specs=pl.BlockSpec((tm, d), lambda i: (i, 0)),
        compiler_params=_params("parallel"),
        name="proj_out",
    )(x2, *ys, w)


def _silu(gate):
    return gate / (1.0 + jnp.exp(-gate))


def _ffn_kernel(x_ref, g_ref, wg_ref, wu_ref, wd_ref, o_ref, h_ref, acc_ref):
    f = pl.program_id(1)

    @pl.when(f == 0)
    def _():
        x = x_ref[...]
        h_ref[...] = _rms(x, g_ref[...]).astype(BF16)
        acc_ref[...] = x

    h = h_ref[...]
    gate = jnp.dot(h, wg_ref[...], preferred_element_type=F32)
    up = jnp.dot(h, wu_ref[...], preferred_element_type=F32)
    act = (_silu(gate) * up).astype(BF16)
    acc_ref[...] += jnp.dot(act, wd_ref[...], preferred_element_type=F32)

    @pl.when(f == pl.num_programs(1) - 1)
    def _():
        o_ref[...] = acc_ref[...]


def _ffn(x2, g, wg, wu, wd, tm, tf):
    n, d = x2.shape
    ff = wg.shape[1]
    return pl.pallas_call(
        _ffn_kernel,
        out_shape=jax.ShapeDtypeStruct((n, d), F32),
        grid=(n // tm, ff // tf),
        in_specs=[pl.BlockSpec((tm, d), lambda i, f: (i, 0)),
                  pl.BlockSpec((1, d), lambda i, f: (0, 0)),
                  pl.BlockSpec((d, tf), lambda i, f: (0, f)),
                  pl.BlockSpec((d, tf), lambda i, f: (0, f)),
                  pl.BlockSpec((tf, d), lambda i, f: (f, 0))],
        out_specs=pl.BlockSpec((tm, d), lambda i, f: (i, 0)),
        scratch_shapes=[pltpu.VMEM((tm, d), BF16), pltpu.VMEM((tm, d), F32)],
        compiler_params=_params("parallel", "arbitrary"),
        name="ffn_dense",
    )(x2, g, wg, wu, wd)


def _router_kernel(x_ref, g_ref, wr_ref, h_ref, route_ref, cstart_ref, ctotal_ref, carry_ref):
    @pl.when(pl.program_id(0) == 0)
    def _():
        carry_ref[...] = jnp.zeros_like(carry_ref)

    h = _rms(x_ref[...], g_ref[...])
    h_ref[...] = h.astype(BF16)
    logits = jnp.dot(h, wr_ref[...], precision=lax.Precision.HIGHEST, preferred_element_type=F32)
    lane = lax.broadcasted_iota(I32, logits.shape, 1)
    lg = jnp.where(lane < N_EXPERTS, logits, -jnp.inf)
    m1 = jnp.max(lg, axis=1, keepdims=True)
    i1 = jnp.min(jnp.where(lg == m1, lane, LANES), axis=1, keepdims=True)
    lg2 = jnp.where(lane == i1, -jnp.inf, lg)
    m2 = jnp.max(lg2, axis=1, keepdims=True)
    i2 = jnp.min(jnp.where(lg2 == m2, lane, LANES), axis=1, keepdims=True)
    e = jnp.exp(m2 - m1)
    w1 = 1.0 / (1.0 + e)
    w2 = e / (1.0 + e)
    tm = lg.shape[0]
    chosen = jnp.where(lane == i1, 1.0, jnp.where(lane == i2, 1.0, 0.0))
    row = lax.broadcasted_iota(I32, (tm, tm), 0)
    col = lax.broadcasted_iota(I32, (tm, tm), 1)
    tri = jnp.where(col < row, 1.0, 0.0).astype(BF16)
    carry = carry_ref[...]
    before = jnp.dot(tri, chosen.astype(BF16), preferred_element_type=F32) + carry
    r1 = jnp.sum(jnp.where(lane == i1, before, 0.0), axis=1, keepdims=True)
    r2 = jnp.sum(jnp.where(lane == i2, before, 0.0), axis=1, keepdims=True)
    vals = (i1.astype(F32), i2.astype(F32), w1, w2, r1, r2)
    route = jnp.zeros_like(lg)
    for idx, val in enumerate(vals):
        route = jnp.where(lane == idx, val, route)
    route_ref[...] = route
    after = carry + jnp.sum(chosen, axis=0, keepdims=True)
    cstart_ref[0] = jnp.broadcast_to(carry, cstart_ref.shape[1:])
    ctotal_ref[...] = jnp.broadcast_to(after, ctotal_ref.shape)
    carry_ref[...] = after


def _router(x2, g, w_router, tm):
    n, d = x2.shape
    return pl.pallas_call(
        _router_kernel,
        out_shape=(jax.ShapeDtypeStruct((n, d), BF16), jax.ShapeDtypeStruct((n, LANES), F32),
                   jax.ShapeDtypeStruct((n // tm, 8, LANES), F32), jax.ShapeDtypeStruct((8, LANES), F32)),
        grid=(n // tm,),
        in_specs=[pl.BlockSpec((tm, d), lambda i: (i, 0)),
                  pl.BlockSpec((1, d), lambda i: (0, 0)),
                  pl.BlockSpec((d, LANES), lambda i: (0, 0))],
        out_specs=(pl.BlockSpec((tm, d), lambda i: (i, 0)), pl.BlockSpec((tm, LANES), lambda i: (i, 0)),
                   pl.BlockSpec((1, 8, LANES), lambda i: (i, 0, 0)), pl.BlockSpec((8, LANES), lambda i: (0, 0))),
        scratch_shapes=[pltpu.VMEM((1, LANES), F32)],
        compiler_params=_params("arbitrary"),
        name="moe_router",
    )(x2, g, w_router)


def _dispatch_kernel(vtile_ref, vchunk_ref, vfirst_ref, vlo_ref, vhi_ref, nvis_ref,
                     pos1_ref, pos2_ref, h_ref, o_ref):
    v = pl.program_id(0)

    @pl.when(v < nvis_ref[0])
    def _():
        ts, tc = o_ref.shape[0], h_ref.shape[0]
        first = vfirst_ref[v] == 1
        lo, hi = vlo_ref[v], vhi_ref[v]
        for blk in range(ts // SUB_DISPATCH):
            r0 = blk * SUB_DISPATCH
            rows = slice(r0, r0 + SUB_DISPATCH)
            active = (lo < r0 + SUB_DISPATCH) & (hi > r0)

            def gathered(r0=r0):
                slot = vtile_ref[v] * ts + r0 + lax.broadcasted_iota(I32, (SUB_DISPATCH, tc), 0)
                hit = jnp.where(slot == pos1_ref[...], 1.0, jnp.where(slot == pos2_ref[...], 1.0, 0.0))
                return jnp.dot(hit.astype(BF16), h_ref[...], preferred_element_type=F32).astype(o_ref.dtype)

            @pl.when(active & first)
            def _():
                o_ref[rows, :] = gathered()

            @pl.when(active & jnp.logical_not(first))
            def _():
                o_ref[rows, :] = o_ref[rows, :] + gathered()

            @pl.when(jnp.logical_not(active) & first)
            def _():
                o_ref[rows, :] = jnp.zeros((SUB_DISPATCH, o_ref.shape[1]), o_ref.dtype)


def _dispatch(h, pos1, pos2, visits, n_slots, ts, tc):
    n, d = h.shape
    nvis = visits[-1]
    imap = lambda f: (lambda v, vt, vc, vf, vlo, vhi, nv: f(vt, vc, v))
    grid_spec = pltpu.PrefetchScalarGridSpec(
        num_scalar_prefetch=len(visits),
        grid=(visits[0].shape[0],),
        in_specs=[pl.BlockSpec((1, tc), imap(lambda vt, vc, v: (0, vc[v]))),
                  pl.BlockSpec((1, tc), imap(lambda vt, vc, v: (0, vc[v]))),
                  pl.BlockSpec((tc, d), imap(lambda vt, vc, v: (vc[v], 0)))],
        out_specs=pl.BlockSpec((ts, d), imap(lambda vt, vc, v: (vt[v], 0))),
    )
    return pl.pallas_call(
        _dispatch_kernel,
        out_shape=jax.ShapeDtypeStruct((n_slots, d), BF16),
        grid_spec=grid_spec,
        compiler_params=_params("arbitrary"),
        name="moe_dispatch",
    )(*visits, pos1, pos2, h)


def _expert_ffn_kernel(texp_ref, nused_ref, xs_ref, wg_ref, wu_ref, wd_ref, o_ref, acc_ref):
    i = pl.program_id(0)
    f = pl.program_id(1)

    @pl.when(f == 0)
    def _():
        acc_ref[...] = jnp.zeros_like(acc_ref)

    @pl.when(i < nused_ref[0])
    def _():
        xs = xs_ref[...]
        gate = jnp.dot(xs, wg_ref[0], preferred_element_type=F32)
        up = jnp.dot(xs, wu_ref[0], preferred_element_type=F32)
        act = (_silu(gate) * up).astype(BF16)
        acc_ref[...] += jnp.dot(act, wd_ref[0], preferred_element_type=F32)

    @pl.when(f == pl.num_programs(1) - 1)
    def _():
        o_ref[...] = acc_ref[...].astype(o_ref.dtype)


def _expert_ffn(xs, tile_expert, n_used, wg, wu, wd, ts, tf):
    n_slots, d = xs.shape
    ff = wg.shape[2]

    def fchunk(i, f, nu):
        return jnp.where(i < nu[0], f, 0)

    grid_spec = pltpu.PrefetchScalarGridSpec(
        num_scalar_prefetch=2,
        grid=(n_slots // ts, ff // tf),
        in_specs=[pl.BlockSpec((ts, d), lambda i, f, te, nu: (i, 0)),
                  pl.BlockSpec((1, d, tf), lambda i, f, te, nu: (te[i], 0, fchunk(i, f, nu))),
                  pl.BlockSpec((1, d, tf), lambda i, f, te, nu: (te[i], 0, fchunk(i, f, nu))),
                  pl.BlockSpec((1, tf, d), lambda i, f, te, nu: (te[i], fchunk(i, f, nu), 0))],
        out_specs=pl.BlockSpec((ts, d), lambda i, f, te, nu: (i, 0)),
        scratch_shapes=[pltpu.VMEM((ts, d), F32)],
    )
    return pl.pallas_call(
        _expert_ffn_kernel,
        out_shape=jax.ShapeDtypeStruct((n_slots, d), BF16),
        grid_spec=grid_spec,
        compiler_params=_params("arbitrary", "arbitrary"),
        name="moe_expert_ffn",
    )(tile_expert, n_used, xs, wg, wu, wd)


def _combine_kernel(seg_row_ref, seg_blocks_ref, seg_first_ref, seg_end_ref,
                    route_ref, gstart_ref, ys_hbm, x_ref, gf_ref, o_ref,
                    acc_ref, cols_ref, ybuf_ref, sem_ref):
    c = pl.program_id(0)
    n_chunks = pl.num_programs(0)
    tc = x_ref.shape[0]

    def block_copy(chunk, buf, e, k, blk):
        row0 = pl.multiple_of(seg_row_ref[chunk * N_EXPERTS + e] + k * COMBINE_ROWS, ROW_ALIGN)
        return pltpu.make_async_copy(ys_hbm.at[pl.ds(row0, COMBINE_ROWS), :], ybuf_ref.at[buf, blk],
                                     sem_ref.at[buf])

    def for_each_block(chunk, fn):
        blk = 0
        for e in range(N_EXPERTS):
            n_blk = seg_blocks_ref[chunk * N_EXPERTS + e]
            for k in range(COMBINE_MAX_BLOCKS):
                pl.when(k < n_blk)(functools.partial(fn, e, k, blk + k))
            blk = blk + n_blk

    def start_reads(chunk, buf):
        for_each_block(chunk, lambda e, k, blk: block_copy(chunk, buf, e, k, blk).start())

    @pl.when(c == 0)
    def _():
        start_reads(0, 0)

    @pl.when(c + 1 < n_chunks)
    def _():
        start_reads(c + 1, (c + 1) % 2)

    buf = c % 2
    acc_ref[...] = x_ref[...]
    route = route_ref[...]
    lane = lax.broadcasted_iota(I32, (1, LANES), 1).astype(F32)
    for which in range(TOP_K):
        start = jnp.sum(jnp.where(lane == route[:, which:which + 1], gstart_ref[...], 0.0),
                        axis=1, keepdims=True)
        pos = start + route[:, 4 + which:5 + which]
        cols_ref[which] = jnp.broadcast_to(pos, (tc, LANES))
        cols_ref[TOP_K + which] = jnp.broadcast_to(route[:, 2 + which:3 + which], (tc, LANES))

    rep = COMBINE_ROWS // LANES
    wide = lambda idx: jnp.concatenate([cols_ref[idx]] * rep, axis=1)

    for_each_block(c, lambda e, k, blk: block_copy(c, buf, e, k, blk).wait())

    def gather_block(e, k, blk):
        row0 = seg_row_ref[c * N_EXPERTS + e] + k * COMBINE_ROWS
        slot = row0 + lax.broadcasted_iota(I32, (1, COMBINE_ROWS), 1)
        inside = (slot >= seg_first_ref[c * N_EXPERTS + e]) & (slot < seg_end_ref[c * N_EXPERTS + e])
        slot = jnp.where(inside, slot, -1).astype(F32)
        gmat = jnp.where(slot == wide(0), wide(2), jnp.where(slot == wide(1), wide(3), 0.0))
        acc_ref[...] += jnp.dot(gmat.astype(BF16), ybuf_ref[buf, blk], preferred_element_type=F32)

    for_each_block(c, gather_block)
    o_ref[...] = _rms(acc_ref[...], gf_ref[...])


def _combine(x2, ys, route, gstart_row, g_final, segs, tc):
    n, d = x2.shape
    max_blocks = (TOP_K * tc + N_EXPERTS * (ROW_ALIGN - 1)) // COMBINE_ROWS + N_EXPERTS
    grid_spec = pltpu.PrefetchScalarGridSpec(
        num_scalar_prefetch=len(segs),
        grid=(n // tc,),
        in_specs=[pl.BlockSpec((tc, LANES), lambda c, *_: (c, 0)),
                  pl.BlockSpec((1, LANES), lambda c, *_: (0, 0)),
                  pl.BlockSpec(memory_space=pl.ANY),
                  pl.BlockSpec((tc, d), lambda c, *_: (c, 0)),
                  pl.BlockSpec((1, d), lambda c, *_: (0, 0))],
        out_specs=pl.BlockSpec((tc, d), lambda c, *_: (c, 0)),
        scratch_shapes=[pltpu.VMEM((tc, d), F32), pltpu.VMEM((2 * TOP_K, tc, LANES), F32),
                        pltpu.VMEM((2, max_blocks, COMBINE_ROWS, d), BF16),
                        pltpu.SemaphoreType.DMA((2,))],
    )
    return pl.pallas_call(
        _combine_kernel,
        out_shape=jax.ShapeDtypeStruct((n, d), F32),
        grid_spec=grid_spec,
        compiler_params=_params("arbitrary"),
        name="moe_combine",
    )(*segs, route, gstart_row, ys, x2, g_final)


def _count_le(ends, v):
    return jnp.sum((ends[None, :] <= v[:, None]).astype(I32), axis=1)


def _flatten_visits(counts, n_max):
    ends = jnp.cumsum(counts)
    total = ends[-1]
    v = jnp.minimum(jnp.arange(n_max, dtype=I32), total - 1)
    seg = _count_le(ends, v)
    within = v - (ends[seg] - counts[seg])
    return seg, within, (within == 0).astype(I32), total.astype(I32)


def _moe_top2(x2, g, w_router, wg, wu, wd, g_final):
    n, d = x2.shape
    ts, tc = T_SLOT, min(T_CHUNK, n)
    n_chunks = n // tc
    n_tiles = TOP_K * n // ts + N_EXPERTS + 1
    n_slots = n_tiles * ts
    n_visits = n_tiles + N_EXPERTS * n_chunks

    h, route, cstart, ctotal = _router(x2, g, w_router, tc)

    experts = jnp.arange(N_EXPERTS, dtype=I32)
    counts = ctotal[0, :N_EXPERTS].astype(I32)
    padded = (counts + ts - 1) // ts * ts
    group_end = jnp.cumsum(padded)
    group_start = group_end - padded
    n_used = (group_end[-1] // ts).astype(I32)
    tiles = jnp.arange(n_tiles, dtype=I32)
    tile_expert = jnp.minimum(_count_le(group_end // ts, tiles), N_EXPERTS - 1)
    c_before = cstart[:, 0, :N_EXPERTS].astype(I32)
    c_after = jnp.concatenate([c_before[1:], counts[None, :]], axis=0)

    e1, e2 = route[:, 0].astype(I32), route[:, 1].astype(I32)
    start_of = lambda e: jnp.sum(jnp.where(e[:, None] == experts, group_start, 0), axis=1)
    pos1 = start_of(e1) + route[:, 4].astype(I32)
    pos2 = start_of(e2) + route[:, 5].astype(I32)

    rank_lo = tiles * ts - group_start[tile_expert]
    rank_hi = jnp.minimum(rank_lo + ts, counts[tile_expert]) - 1
    after_t = c_after[:, tile_expert]
    used = tiles < n_used
    c_lo = jnp.where(used, jnp.sum((after_t <= rank_lo).astype(I32), axis=0), 0)
    c_hi = jnp.where(used, jnp.sum((after_t <= rank_hi).astype(I32), axis=0), 0)
    d_tile, d_within, d_first, d_total = _flatten_visits(c_hi - c_lo + 1, n_visits)
    d_chunk = c_lo[d_tile] + d_within

    def slot_range(chunk, expert, tile):
        lo = group_start[expert] + c_before[chunk, expert] - tile * ts
        hi = group_start[expert] + c_after[chunk, expert] - tile * ts
        return jnp.clip(lo, 0, ts), jnp.clip(hi, 0, ts)

    d_lo, d_hi = slot_range(d_chunk, tile_expert[d_tile], d_tile)
    d_real = (d_tile < n_used).astype(I32)
    d_visits = (d_tile, d_chunk, d_first, d_lo * d_real, d_hi * d_real, d_total.reshape(1))

    seg_first = group_start + c_before
    seg_row = seg_first // ROW_ALIGN * ROW_ALIGN
    seg_blocks = jnp.where(c_after > c_before,
                           (seg_first - seg_row + c_after - c_before + COMBINE_ROWS - 1) // COMBINE_ROWS, 0)

    xs = _dispatch(h, pos1.reshape(1, n), pos2.reshape(1, n), d_visits, n_slots, ts, tc)
    ys = _expert_ffn(xs, tile_expert, n_used.reshape(1), wg, wu, wd, ts, min(TF_EXPERT, wg.shape[2]))
    gstart_row = _pad_cols(group_start.astype(F32).reshape(1, N_EXPERTS), LANES)
    segs = tuple(t.reshape(-1) for t in (seg_row, seg_blocks, seg_first, seg_first + c_after - c_before))
    return _combine(x2, ys, route, gstart_row, g_final, segs, tc)


def _pad_cols(w, width):
    return jnp.pad(w, ((0, 0), (0, width - w.shape[1])))


def _prep_w_in(w):
    parts = []
    off = 0
    for size in IN_SIZES:
        parts.append(w[:, off:off + size])
        off += size
    cq, ckv, kr, q_b, k_b, v_b, q_c, k_c, v_c, f_c, q_d, k_d, v_d = parts
    d = w.shape[0]
    kr_blk = jnp.concatenate([jnp.zeros((d, MLA_NOPE), w.dtype), kr,
                              jnp.zeros((d, LANES - MLA_NOPE - MLA_ROPE), w.dtype)], axis=1)
    cols = [cq, ckv, kr_blk, _pad_cols(f_c, LANES), q_b, k_b, v_b, q_c, k_c, v_c, q_d, k_d, v_d]
    return jnp.concatenate(cols, axis=1).astype(BF16)


def _prep_mla(w_uq, w_ukv):
    qk = MLA_NOPE + MLA_ROPE
    wuq = jnp.pad(w_uq.reshape(MLA_Q_RANK, GROUP_HEADS, qk), ((0, 0), (0, 0), (0, LANES - qk)))
    wuq = wuq.reshape(MLA_Q_RANK, GROUP_HEADS * LANES)
    kv = w_ukv.reshape(MLA_KV_RANK, GROUP_HEADS, MLA_NOPE + HEAD_DIM)
    wuk = jnp.pad(kv[:, :, :MLA_NOPE], ((0, 0), (0, 0), (0, LANES - MLA_NOPE)))
    wuk = wuk.reshape(MLA_KV_RANK, GROUP_HEADS * LANES)
    wuv = kv[:, :, MLA_NOPE:].reshape(MLA_KV_RANK, GROUP_WIDTH)
    return wuq.astype(BF16), wuk.astype(BF16), wuv.astype(BF16)


def _row(v, width=None):
    v = v.reshape(1, -1).astype(F32)
    return v if width is None else _pad_cols(v, width)


def kernel(x, g_mix, w_in, b_forget, g_q_lat, g_kv_lat, w_uq, w_ukv, lambda_q1, lambda_k1, lambda_q2,
           lambda_k2, g_diff_sub, w_out, g_ffn, w_ffn_gate, w_ffn_up, w_ffn_down, w_router, w_exp_gate,
           w_exp_up, w_exp_down, g_final):
    b, s_len, d = x.shape
    depth = g_mix.shape[0]
    assert d == D_MODEL and depth == 2, "kernel is specialised to the two-layer trunk"
    assert w_router.shape[-1] == N_EXPERTS
    n = b * s_len
    t_attn = T_ATTN
    tm_in = min(TM_PROJ, s_len)
    tm_tok = min(TM_TOK, n)
    assert s_len % t_attn == 0 and s_len % tm_in == 0 and n % tm_tok == 0
    assert n % min(T_CHUNK, n) == 0 and (TOP_K * n) % T_SLOT == 0

    tabs = _rope_tables(s_len)
    log_mult = _dilated_log2_multiplicity(t_attn)
    n_lm = log_mult.shape[0]

    for l in range(depth):
        wuq, wuk, wuv = _prep_mla(w_uq[l], w_ukv[l])
        zz, vt, zf = _proj_in(x, _row(g_mix[l]), _prep_w_in(w_in[l]), _row(g_q_lat[l]),
                              _row(g_kv_lat[l]), wuq, wuk, wuv, tabs, tm_in)
        crep, crow = _forget_cumsum(zf, _row(b_forget[l], LANES), tm_in)

        lam_init = 0.8 - 0.6 * math.exp(-0.3 * l)
        lamv = jnp.concatenate(
            [_row(v, LANES) for v in (lambda_q1[l], lambda_k1[l], lambda_q2[l], lambda_k2[l])]
            + [jnp.zeros((4, LANES), F32)], axis=0)
        gsub = jnp.tile(_row(g_diff_sub[l]), (1, GROUP_HEADS))

        y_a = _attention(zz, vt, 0, "mla", t_attn, ZZ_QA, ZZ_KA)
        c0 = ZZ_QB
        y_b = _attention(zz, vt, 1, "dil", t_attn, c0, c0 + GROUP_WIDTH, extra=(log_mult,),
                         extra_specs=(pl.BlockSpec((n_lm, t_attn, t_attn), lambda bi, qi: (0, 0, 0)),))
        c0 += 2 * GROUP_WIDTH
        y_c = _attention(zz, vt, 2, "fox", t_attn, c0, c0 + GROUP_WIDTH, extra=(crep, crow),
                         extra_specs=(pl.BlockSpec((1, GROUP_HEADS, s_len, LANES), lambda bi, qi: (bi, 0, 0, 0)),
                                      pl.BlockSpec((1, 8, s_len), lambda bi, qi: (bi, 0, 0))))
        c0 += 2 * GROUP_WIDTH
        y_d = _attention(zz, vt, 3, "diff", t_attn, c0, c0 + GROUP_WIDTH, extra=(lamv, gsub),
                         extra_specs=(pl.BlockSpec((8, LANES), lambda bi, qi: (0, 0)),
                                      pl.BlockSpec((1, GROUP_WIDTH), lambda bi, qi: (0, 0))),
                         lam_init=lam_init)

        ys = [y.reshape(n, GROUP_WIDTH) for y in (y_a, y_b, y_c, y_d)]
        x2 = _proj_out(x.reshape(n, d), ys, w_out[l].astype(BF16), tm_tok)

        i = l // 2
        if l % 2 == 0:
            x2 = _ffn(x2, _row(g_ffn[l]), w_ffn_gate[i].astype(BF16), w_ffn_up[i].astype(BF16),
                      w_ffn_down[i].astype(BF16), tm_tok, TF_DENSE)
        else:
            x2 = _moe_top2(x2, _row(g_ffn[l]), _pad_cols(w_router[i].astype(F32), LANES),
                           w_exp_gate[i].astype(BF16), w_exp_up[i].astype(BF16),
                           w_exp_down[i].astype(BF16), _row(g_final))
        x = x2.reshape(b, s_len, d)
    return x
```

```python
import functools
import math

import jax
import jax.numpy as jnp
from jax import lax
from jax.experimental import pallas as pl
from jax.experimental.pallas import tpu as pltpu

F32 = jnp.float32
BF16 = jnp.bfloat16
I32 = jnp.int32

D_MODEL = 1024
HEAD_DIM = 64
GROUP_HEADS = 4
GROUP_WIDTH = GROUP_HEADS * HEAD_DIM
N_MIXERS = 4
MLA_Q_RANK = 256
MLA_KV_RANK = 128
MLA_NOPE = 64
MLA_ROPE = 32
DIL_CONFIGS = ((128, 1), (512, 4), (2048, 16))
DIL_MAX_WINDOW = max(w for w, _ in DIL_CONFIGS)
DIFF_QK = HEAD_DIM // 2
N_EXPERTS = 8
TOP_K = 2
ROPE_THETA = 10000.0
NORM_EPS = 1e-6
NEG_INF = -1e30
LOG2E = math.log2(math.e)
ACC_ROWS = HEAD_DIM + 16
LANES = 128
IN_SIZES = (MLA_Q_RANK, MLA_KV_RANK, MLA_ROPE,
            GROUP_WIDTH, GROUP_WIDTH, GROUP_WIDTH,
            GROUP_WIDTH, GROUP_WIDTH, GROUP_WIDTH, GROUP_HEADS,
            GROUP_WIDTH, GROUP_WIDTH, GROUP_WIDTH)

WP_CQ, WP_CKV, WP_KR, WP_F = 0, 256, 384, 512
WP_QB = 640
WP_WIDTH = WP_QB + 9 * GROUP_WIDTH

ZZ_QA, ZZ_KA = 0, 512
ZZ_QB = 1024
ZZ_WIDTH = ZZ_QB + 6 * GROUP_WIDTH

ROPE_A, ROPE_B, ROPE_D = 0, 1, 2

VMEM_LIMIT = 56 * 1024 * 1024

T_ATTN = 256
TM_PROJ = 512
TM_TOK = 1024
TF_DENSE = 256
T_SLOT = 512
T_CHUNK = 512
TF_EXPERT = 1792
SUB_DISPATCH = 128
ROW_ALIGN = 16
COMBINE_ROWS = 256
COMBINE_MAX_BLOCKS = -(-(T_CHUNK + ROW_ALIGN - 1) // COMBINE_ROWS)


def _params(*sem):
    return pltpu.CompilerParams(dimension_semantics=sem, vmem_limit_bytes=VMEM_LIMIT)


def _rope_tables(s_len):
    pos = jnp.arange(s_len, dtype=F32)[:, None]
    lane = jnp.arange(LANES)[None, :]

    def kind(rel, width, active):
        half = width // 2
        idx = (rel % half).astype(F32)
        inv = ROPE_THETA ** (-idx / half)
        ang = pos * inv
        cos, sin = jnp.cos(ang), jnp.sin(ang)
        first = active & (rel < half)
        second = active & (rel >= half)
        c = jnp.where(active, cos, 1.0)
        sm = jnp.where(first, -sin, 0.0)
        sp = jnp.where(second, sin, 0.0)
        return [c, sm, sp]

    a_active = (lane >= MLA_NOPE) & (lane < MLA_NOPE + MLA_ROPE)
    tabs = (kind(jnp.where(a_active, lane - MLA_NOPE, 0), MLA_ROPE, a_active)
            + kind(lane % HEAD_DIM, HEAD_DIM, lane >= 0)
            + kind(lane % DIFF_QK, DIFF_QK, lane >= 0))
    return jnp.stack([jnp.broadcast_to(t, (s_len, LANES)).astype(F32) for t in tabs])


_ROPE_HALF = {ROPE_A: MLA_ROPE // 2, ROPE_B: HEAD_DIM // 2, ROPE_D: DIFF_QK // 2}


def _rope(blk, tab_ref, kind):
    half = _ROPE_HALF[kind]
    c = tab_ref[3 * kind]
    sm = tab_ref[3 * kind + 1]
    sp = tab_ref[3 * kind + 2]
    return blk * c + pltpu.roll(blk, LANES - half, 1) * sm + pltpu.roll(blk, half, 1) * sp


def _rms(x, g):
    return x * lax.rsqrt(jnp.mean(x * x, axis=-1, keepdims=True) + NORM_EPS) * g


def _proj_in_kernel(x_ref, g_ref, w_ref, gql_ref, gkvl_ref, wuq_ref, wuk_ref, wuv_ref, tab_ref,
                    zz_ref, vt_ref, zf_ref):
    h = _rms(x_ref[0], g_ref[...]).astype(BF16)

    def proj(c0, width):
        return jnp.dot(h, w_ref[:, c0:c0 + width], preferred_element_type=F32)

    def put(col, val):
        zz_ref[0, :, col:col + LANES] = val.astype(BF16)

    def put_v(mixer, val):
        vt_ref[0, mixer * GROUP_WIDTH:(mixer + 1) * GROUP_WIDTH, :] = val.T.astype(BF16)

    latent = {}
    scale_a = (MLA_NOPE + MLA_ROPE) ** -0.5 * LOG2E
    plan = (((ROPE_B, HEAD_DIM ** -0.5 * LOG2E), (ROPE_B, 1.0)),
            ((None, HEAD_DIM ** -0.5 * LOG2E), (None, 1.0)),
            ((ROPE_D, DIFF_QK ** -0.5 * LOG2E), (ROPE_D, 1.0)))

    def group(idx):
        return lambda: proj(WP_QB + idx * GROUP_WIDTH, GROUP_WIDTH)

    def qk_epilogue(mx, t, kind, scale, z):
        for half in range(GROUP_WIDTH // LANES):
            blk = z[:, half * LANES:(half + 1) * LANES]
            if kind is not None:
                blk = _rope(blk, tab_ref, kind)
            if scale != 1.0:
                blk = blk * scale
            put(ZZ_QB + (2 * mx + t) * GROUP_WIDTH + half * LANES, blk)

    def mla_epilogue(col, scale, z):
        for hh in range(GROUP_HEADS):
            blk = _rope(z[:, hh * LANES:(hh + 1) * LANES], tab_ref, ROPE_A)
            put(col + hh * LANES, blk * scale if scale != 1.0 else blk)

    def keep(name, g_ref_, z):
        latent[name] = _rms(z, g_ref_[...]).astype(BF16)

    def store_f(z):
        zf_ref[0] = z

    def mixer_tasks(mx):
        out = [(group(3 * mx + t), functools.partial(qk_epilogue, mx, t, kind, scale))
               for t, (kind, scale) in enumerate(plan[mx])]
        return out + [(group(3 * mx + 2), functools.partial(put_v, mx + 1))]

    tasks = [
        (lambda: proj(WP_CQ, MLA_Q_RANK), functools.partial(keep, "cq", gql_ref)),
        (lambda: proj(WP_CKV, MLA_KV_RANK), functools.partial(keep, "ckv", gkvl_ref)),
        *mixer_tasks(0)[:2],
        (lambda: jnp.dot(latent["cq"], wuq_ref[...], preferred_element_type=F32),
         functools.partial(mla_epilogue, ZZ_QA, scale_a)),
        (lambda: jnp.dot(latent["ckv"], wuk_ref[...], preferred_element_type=F32)
         + jnp.concatenate([proj(WP_KR, LANES)] * GROUP_HEADS, axis=1),
         functools.partial(mla_epilogue, ZZ_KA, 1.0)),
        (lambda: jnp.dot(latent["ckv"], wuv_ref[...], preferred_element_type=F32), functools.partial(put_v, 0)),
        (lambda: proj(WP_F, LANES), store_f),
        *mixer_tasks(0)[2:], *mixer_tasks(1), *mixer_tasks(2),
    ]
    z = tasks[0][0]()
    for i, (_, epilogue) in enumerate(tasks):
        z_next = tasks[i + 1][0]() if i + 1 < len(tasks) else None
        epilogue(z)
        z = z_next


def _proj_in(x, g, w, gql, gkvl, wuq, wuk, wuv, tabs, tm):
    b, s_len, d = x.shape
    const = lambda shape: pl.BlockSpec(shape, lambda bi, si: (0,) * len(shape))
    return pl.pallas_call(
        _proj_in_kernel,
        out_shape=(jax.ShapeDtypeStruct((b, s_len, ZZ_WIDTH), BF16),
                   jax.ShapeDtypeStruct((b, N_MIXERS * GROUP_WIDTH, s_len), BF16),
                   jax.ShapeDtypeStruct((b, s_len, LANES), F32)),
        grid=(b, s_len // tm),
        in_specs=[
            pl.BlockSpec((1, tm, d), lambda bi, si: (bi, si, 0)),
            const((1, d)), const((d, WP_WIDTH)),
            const((1, MLA_Q_RANK)), const((1, MLA_KV_RANK)),
            const((MLA_Q_RANK, GROUP_HEADS * LANES)),
            const((MLA_KV_RANK, GROUP_HEADS * LANES)),
            const((MLA_KV_RANK, GROUP_WIDTH)),
            pl.BlockSpec((9, tm, LANES), lambda bi, si: (0, si, 0)),
        ],
        out_specs=(pl.BlockSpec((1, tm, ZZ_WIDTH), lambda bi, si: (bi, si, 0)),
                   pl.BlockSpec((1, N_MIXERS * GROUP_WIDTH, tm), lambda bi, si: (bi, 0, si)),
                   pl.BlockSpec((1, tm, LANES), lambda bi, si: (bi, si, 0))),
        compiler_params=_params("parallel", "parallel"),
        name="proj_in",
    )(x, g, w, gql, gkvl, wuq, wuk, wuv, tabs)


def _forget_cumsum_kernel(zf_ref, bias_ref, crep_ref, crow_ref, carry_ref):
    @pl.when(pl.program_id(1) == 0)
    def _():
        carry_ref[...] = jnp.zeros_like(carry_ref)

    t = zf_ref[0] + bias_ref[...]
    log_f = jnp.minimum(t, 0.0) - jnp.log(1.0 + jnp.exp(-jnp.abs(t)))
    tc = log_f.shape[0]
    row = lax.broadcasted_iota(I32, (tc, tc), 0)
    col = lax.broadcasted_iota(I32, (tc, tc), 1)
    tri = jnp.where(col <= row, 1.0, 0.0).astype(F32)
    cs = jnp.dot(tri, log_f, precision=lax.Precision.HIGHEST, preferred_element_type=F32) + carry_ref[...]
    carry_ref[...] = cs[tc - 1:tc, :]
    cs2 = cs * LOG2E
    crow_ref[0] = cs2.T[0:8, :]
    for hh in range(GROUP_HEADS):
        crep_ref[0, hh] = jnp.broadcast_to(cs2[:, hh:hh + 1], (tc, LANES))


def _forget_cumsum(zf, bias, tc):
    b, s_len, _ = zf.shape
    return pl.pallas_call(
        _forget_cumsum_kernel,
        out_shape=(jax.ShapeDtypeStruct((b, GROUP_HEADS, s_len, LANES), F32),
                   jax.ShapeDtypeStruct((b, 8, s_len), F32)),
        grid=(b, s_len // tc),
        in_specs=[pl.BlockSpec((1, tc, LANES), lambda bi, si: (bi, si, 0)),
                  pl.BlockSpec((1, LANES), lambda bi, si: (0, 0))],
        out_specs=(pl.BlockSpec((1, GROUP_HEADS, tc, LANES), lambda bi, si: (bi, 0, si, 0)),
                   pl.BlockSpec((1, 8, tc), lambda bi, si: (bi, 0, si))),
        scratch_shapes=[pltpu.VMEM((1, LANES), F32)],
        compiler_params=_params("parallel", "arbitrary"),
        name="forget_cumsum",
    )(zf, bias)


def _dilated_log2_multiplicity(t):
    n_tiles = DIL_MAX_WINDOW // t + 1
    k = jnp.arange(n_tiles)[:, None, None]
    j = jnp.arange(t)[None, :, None]
    i = jnp.arange(t)[None, None, :]
    delta = k * t + i - j
    mult = jnp.zeros(delta.shape, F32)
    for window, dilation in DIL_CONFIGS:
        mult = mult + ((delta >= 0) & (delta <= window) & (delta % dilation == 0)).astype(F32)
    return jnp.where(mult > 0, jnp.log2(jnp.maximum(mult, 1.0)), NEG_INF).astype(F32)


def _attn_kernel(*refs, mixer, t, lam_init):
    q_ref, k_ref, vt_ref = refs[:3]
    o_ref, acc_ref, sa_ref, sb_ref = refs[-4:]
    extra = refs[3:-4]
    qi = pl.program_id(1)
    sep = mixer == "mla"
    nsub = 2 * GROUP_HEADS if mixer == "diff" else GROUP_HEADS
    per_grp = 1 if sep else nsub // 2
    sub_w = LANES // per_grp

    lane = lax.broadcasted_iota(I32, (1, LANES), 1)
    q = q_ref[0]
    qs = []
    for hh in range(nsub):
        grp, pos = hh // per_grp, hh % per_grp
        qg = q[:, grp * LANES:(grp + 1) * LANES]
        if not sep:
            qg = jnp.where((lane >= pos * sub_w) & (lane < (pos + 1) * sub_w), qg, jnp.zeros_like(qg))
        qs.append(qg)

    if mixer == "fox":
        crep_ref, crow_ref = extra
        q_off = pl.multiple_of(qi * t, t)
        cq = [crow_ref[0, hh:hh + 1, pl.ds(q_off, t)] for hh in range(nsub)]
    if mixer == "dil":
        (lm_ref,) = extra

    key_idx = lax.broadcasted_iota(I32, (t, t), 0)
    qry_idx = lax.broadcasted_iota(I32, (t, t), 1)
    causal = key_idx <= qry_idx

    acc_ref[...] = jnp.zeros_like(acc_ref)

    def scores(j, s_ref):
        off = pl.multiple_of(j * t, t)
        k = k_ref[0, pl.ds(off, t), :]
        for hh in range(nsub):
            grp = hh // per_grp
            s_ref[hh] = lax.dot_general(k[:, grp * LANES:(grp + 1) * LANES], qs[hh],
                                        (((1,), (1,)), ((), ())), preferred_element_type=F32)

    def consume(j, s_ref, ms, diag):
        off = pl.multiple_of(j * t, t)
        ss = []
        for hh in range(nsub):
            s = s_ref[hh]
            if mixer == "fox":
                ck = crep_ref[0, hh, pl.ds(off, t), :]
                s = s - jnp.concatenate([ck] * (t // LANES), axis=1)
            if mixer == "dil":
                s = s + lm_ref[qi - j]
            elif diag:
                s = jnp.where(causal, s, NEG_INF)
            ss.append(s)
        if mixer == "fox":
            new_m = [jnp.maximum(ms[hh], jnp.max(ss[hh], axis=0, keepdims=True) + cq[hh]) for hh in range(nsub)]
            ps = [jnp.exp2(ss[hh] - (new_m[hh] - cq[hh])).astype(BF16) for hh in range(nsub)]
        else:
            new_m = [jnp.maximum(ms[hh], jnp.max(ss[hh], axis=0, keepdims=True)) for hh in range(nsub)]
            ps = [jnp.exp2(ss[hh] - new_m[hh]).astype(BF16) for hh in range(nsub)]
        alphas = [jnp.exp2(ms[hh] - new_m[hh]) for hh in range(nsub)]
        pvs = []
        for hh in range(nsub):
            vh = hh * GROUP_HEADS // nsub
            vt = vt_ref[0, vh * HEAD_DIM:(vh + 1) * HEAD_DIM, pl.ds(off, t)]
            vt1 = jnp.concatenate([vt, jnp.ones((ACC_ROWS - HEAD_DIM, t), BF16)], axis=0)
            pvs.append(jnp.dot(vt1, ps[hh], preferred_element_type=F32))
        for hh in range(nsub):
            acc_ref[hh] = acc_ref[hh] * alphas[hh] + pvs[hh]
        return tuple(new_m)

    def run_tiles(j, count, ms):
        for u in range(0, count, 2):
            scores(j + u + 1, sb_ref)
            ms = consume(j + u, sa_ref, ms, False)
            scores(j + u + 2, sa_ref)
            ms = consume(j + u + 1, sb_ref, ms, False)
        return ms

    lo = jnp.maximum(qi - DIL_MAX_WINDOW // t, 0) if mixer == "dil" else 0
    n_full = qi - lo
    init = tuple(jnp.full((1, t), NEG_INF, F32) for _ in range(nsub))
    scores(lo, sa_ref)
    n_quads = n_full // 4
    ms = lax.fori_loop(0, n_quads, lambda i, ms: run_tiles(lo + 4 * i, 4, ms), init)
    rest = lo + 4 * n_quads
    ms = lax.fori_loop(0, (n_full % 4) // 2, lambda i, ms: run_tiles(rest, 2, ms), ms)

    def odd_tail(ms):
        scores(qi, sb_ref)
        ms = consume(qi - 1, sa_ref, ms, False)
        return consume(qi, sb_ref, ms, True)

    def even_tail(ms):
        return consume(qi, sa_ref, ms, True)

    lax.cond(n_full % 2 == 1, odd_tail, even_tail, ms)

    def head_out(hh):
        a = acc_ref[hh]
        return a[:HEAD_DIM] / a[HEAD_DIM:HEAD_DIM + 1]

    if mixer != "diff":
        o_t = jnp.concatenate([head_out(hh) for hh in range(nsub)], axis=0)
        o_ref[0] = o_t.T.astype(o_ref.dtype)
    else:
        lamv_ref, gsub_ref = extra
        lamv = lamv_ref[...]
        e1 = jnp.exp(jnp.sum(lamv[0:1] * lamv[1:2], axis=1, keepdims=True))
        e2 = jnp.exp(jnp.sum(lamv[2:3] * lamv[3:4], axis=1, keepdims=True))
        lam = e1 - e2 + lam_init
        heads = []
        for vh in range(GROUP_HEADS):
            o = head_out(2 * vh) - lam * head_out(2 * vh + 1)
            ms_h = jnp.mean(o * o, axis=0, keepdims=True)
            heads.append(o * lax.rsqrt(ms_h + NORM_EPS))
        o_t = jnp.concatenate(heads, axis=0)
        o_ref[0] = (o_t.T * gsub_ref[...] * (1.0 - lam_init)).astype(o_ref.dtype)


def _attention(zz, vt, mixer_idx, mixer, t, q_col, k_col, extra=(), extra_specs=(), lam_init=0.0):
    b, s_len, _ = zz.shape
    qw = GROUP_HEADS * LANES if mixer == "mla" else GROUP_WIDTH
    qb, kb = q_col // qw, k_col // qw
    nsub = 2 * GROUP_HEADS if mixer == "diff" else GROUP_HEADS
    return pl.pallas_call(
        functools.partial(_attn_kernel, mixer=mixer, t=t, lam_init=lam_init),
        out_shape=jax.ShapeDtypeStruct((b, s_len, GROUP_WIDTH), BF16),
        grid=(b, s_len // t),
        in_specs=[pl.BlockSpec((1, t, qw), lambda bi, qi: (bi, qi, qb)),
                  pl.BlockSpec((1, s_len, qw), lambda bi, qi: (bi, 0, kb)),
                  pl.BlockSpec((1, GROUP_WIDTH, s_len), lambda bi, qi: (bi, mixer_idx, 0)),
                  *extra_specs],
        out_specs=pl.BlockSpec((1, t, GROUP_WIDTH), lambda bi, qi: (bi, qi, 0)),
        scratch_shapes=[pltpu.VMEM((nsub, ACC_ROWS, t), F32), pltpu.VMEM((nsub, t, t), F32),
                        pltpu.VMEM((nsub, t, t), F32)],
        compiler_params=_params("parallel", "arbitrary"),
        name="attn_" + mixer,
    )(zz, zz, vt, *extra)


def _proj_out_kernel(x_ref, ya_ref, yb_ref, yc_ref, yd_ref, w_ref, o_ref):
    acc = x_ref[...]
    for i, y_ref in enumerate((ya_ref, yb_ref, yc_ref, yd_ref)):
        acc = acc + jnp.dot(y_ref[...], w_ref[i * GROUP_WIDTH:(i + 1) * GROUP_WIDTH, :],
                            preferred_element_type=F32)
    o_ref[...] = acc


def _proj_out(x2, ys, w, tm):
    n, d = x2.shape
    yspec = pl.BlockSpec((tm, GROUP_WIDTH), lambda i: (i, 0))
    return pl.pallas_call(
        _proj_out_kernel,
        out_shape=jax.ShapeDtypeStruct((n, d), F32),
        grid=(n // tm,),
        in_specs=[pl.BlockSpec((tm, d), lambda i: (i, 0)), yspec, yspec, yspec, yspec,
                  pl.BlockSpec((4 * GROUP_WIDTH, d), lambda i: (0, 0))],
        out_specs=pl.BlockSpec((tm, d), lambda i: (i, 0)),
        compiler_params=_params("parallel"),
        name="proj_out",
    )(x2, *ys, w)


def _silu(gate):
    return gate / (1.0 + jnp.exp(-gate))


def _ffn_kernel(x_ref, g_ref, wg_ref, wu_ref, wd_ref, o_ref, h_ref, acc_ref):
    f = pl.program_id(1)

    @pl.when(f == 0)
    def _():
        x = x_ref[...]
        h_ref[...] = _rms(x, g_ref[...]).astype(BF16)
        acc_ref[...] = x

    h = h_ref[...]
    gate = jnp.dot(h, wg_ref[...], preferred_element_type=F32)
    up = jnp.dot(h, wu_ref[...], preferred_element_type=F32)
    act = (_silu(gate) * up).astype(BF16)
    acc_ref[...] += jnp.dot(act, wd_ref[...], preferred_element_type=F32)

    @pl.when(f == pl.num_programs(1) - 1)
    def _():
        o_ref[...] = acc_ref[...]


def _ffn(x2, g, wg, wu, wd, tm, tf):
    n, d = x2.shape
    ff = wg.shape[1]
    return pl.pallas_call(
        _ffn_kernel,
        out_shape=jax.ShapeDtypeStruct((n, d), F32),
        grid=(n // tm, ff // tf),
        in_specs=[pl.BlockSpec((tm, d), lambda i, f: (i, 0)),
                  pl.BlockSpec((1, d), lambda i, f: (0, 0)),
                  pl.BlockSpec((d, tf), lambda i, f: (0, f)),
                  pl.BlockSpec((d, tf), lambda i, f: (0, f)),
                  pl.BlockSpec((tf, d), lambda i, f: (f, 0))],
        out_specs=pl.BlockSpec((tm, d), lambda i, f: (i, 0)),
        scratch_shapes=[pltpu.VMEM((tm, d), BF16), pltpu.VMEM((tm, d), F32)],
        compiler_params=_params("parallel", "arbitrary"),
        name="ffn_dense",
    )(x2, g, wg, wu, wd)


def _router_kernel(x_ref, g_ref, wr_ref, h_ref, route_ref, cstart_ref, ctotal_ref, carry_ref):
    @pl.when(pl.program_id(0) == 0)
    def _():
        carry_ref[...] = jnp.zeros_like(carry_ref)

    h = _rms(x_ref[...], g_ref[...])
    h_ref[...] = h.astype(BF16)
    logits = jnp.dot(h, wr_ref[...], precision=lax.Precision.HIGHEST, preferred_element_type=F32)
    lane = lax.broadcasted_iota(I32, logits.shape, 1)
    lg = jnp.where(lane < N_EXPERTS, logits, -jnp.inf)
    m1 = jnp.max(lg, axis=1, keepdims=True)
    i1 = jnp.min(jnp.where(lg == m1, lane, LANES), axis=1, keepdims=True)
    lg2 = jnp.where(lane == i1, -jnp.inf, lg)
    m2 = jnp.max(lg2, axis=1, keepdims=True)
    i2 = jnp.min(jnp.where(lg2 == m2, lane, LANES), axis=1, keepdims=True)
    e = jnp.exp(m2 - m1)
    w1 = 1.0 / (1.0 + e)
    w2 = e / (1.0 + e)
    tm = lg.shape[0]
    chosen = jnp.where(lane == i1, 1.0, jnp.where(lane == i2, 1.0, 0.0))
    row = lax.broadcasted_iota(I32, (tm, tm), 0)
    col = lax.broadcasted_iota(I32, (tm, tm), 1)
    tri = jnp.where(col < row, 1.0, 0.0).astype(BF16)
    carry = carry_ref[...]
    before = jnp.dot(tri, chosen.astype(BF16), preferred_element_type=F32) + carry
    r1 = jnp.sum(jnp.where(lane == i1, before, 0.0), axis=1, keepdims=True)
    r2 = jnp.sum(jnp.where(lane == i2, before, 0.0), axis=1, keepdims=True)
    vals = (i1.astype(F32), i2.astype(F32), w1, w2, r1, r2)
    route = jnp.zeros_like(lg)
    for idx, val in enumerate(vals):
        route = jnp.where(lane == idx, val, route)
    route_ref[...] = route
    after = carry + jnp.sum(chosen, axis=0, keepdims=True)
    cstart_ref[0] = jnp.broadcast_to(carry, cstart_ref.shape[1:])
    ctotal_ref[...] = jnp.broadcast_to(after, ctotal_ref.shape)
    carry_ref[...] = after


def _router(x2, g, w_router, tm):
    n, d = x2.shape
    return pl.pallas_call(
        _router_kernel,
        out_shape=(jax.ShapeDtypeStruct((n, d), BF16), jax.ShapeDtypeStruct((n, LANES), F32),
                   jax.ShapeDtypeStruct((n // tm, 8, LANES), F32), jax.ShapeDtypeStruct((8, LANES), F32)),
        grid=(n // tm,),
        in_specs=[pl.BlockSpec((tm, d), lambda i: (i, 0)),
                  pl.BlockSpec((1, d), lambda i: (0, 0)),
                  pl.BlockSpec((d, LANES), lambda i: (0, 0))],
        out_specs=(pl.BlockSpec((tm, d), lambda i: (i, 0)), pl.BlockSpec((tm, LANES), lambda i: (i, 0)),
                   pl.BlockSpec((1, 8, LANES), lambda i: (i, 0, 0)), pl.BlockSpec((8, LANES), lambda i: (0, 0))),
        scratch_shapes=[pltpu.VMEM((1, LANES), F32)],
        compiler_params=_params("arbitrary"),
        name="moe_router",
    )(x2, g, w_router)


def _dispatch_kernel(texp_ref, clo_ref, nvis_ref, gstart_ref, cbefore_ref, cafter_ref,
                     pos1_ref, pos2_ref, h_hbm, o_ref, hbuf_ref, sem_ref):
    t = pl.program_id(0)
    ts, tc = o_ref.shape[0], hbuf_ref.shape[1]
    expert, first_chunk, n_vis = texp_ref[t], clo_ref[t], nvis_ref[t]

    def chunk_copy(i, buf):
        row0 = pl.multiple_of((first_chunk + i) * tc, tc)
        return pltpu.make_async_copy(h_hbm.at[pl.ds(row0, tc), :], hbuf_ref.at[buf], sem_ref.at[buf])

    o_ref[...] = jnp.zeros_like(o_ref)

    @pl.when(n_vis > 0)
    def _():
        chunk_copy(0, 0).start()

    def visit(i, carry):
        buf = i % 2

        @pl.when(i + 1 < n_vis)
        def _():
            chunk_copy(i + 1, 1 - buf).start()

        chunk_copy(i, buf).wait()
        chunk = first_chunk + i
        base = gstart_ref[expert] - t * ts
        lo = jnp.clip(base + cbefore_ref[chunk * N_EXPERTS + expert], 0, ts)
        hi = jnp.clip(base + cafter_ref[chunk * N_EXPERTS + expert], 0, ts)
        off = pl.multiple_of(chunk * tc, tc)
        pos1 = pos1_ref[:, pl.ds(off, tc)]
        pos2 = pos2_ref[:, pl.ds(off, tc)]
        for blk in range(ts // SUB_DISPATCH):
            r0 = blk * SUB_DISPATCH

            @pl.when((lo < r0 + SUB_DISPATCH) & (hi > r0))
            def _(r0=r0):
                slot = t * ts + r0 + lax.broadcasted_iota(I32, (SUB_DISPATCH, tc), 0)
                hit = jnp.where(slot == pos1, 1.0, jnp.where(slot == pos2, 1.0, 0.0)).astype(BF16)
                rows = jnp.dot(hit, hbuf_ref[buf], preferred_element_type=F32).astype(o_ref.dtype)
                o_ref[r0:r0 + SUB_DISPATCH, :] = o_ref[r0:r0 + SUB_DISPATCH, :] + rows

        return carry

    lax.fori_loop(0, n_vis, visit, 0)


def _dispatch(h, pos1, pos2, tables, n_slots, ts, tc):
    n, d = h.shape
    grid_spec = pltpu.PrefetchScalarGridSpec(
        num_scalar_prefetch=len(tables),
        grid=(n_slots // ts,),
        in_specs=[pl.BlockSpec((1, n), lambda t, *_: (0, 0)),
                  pl.BlockSpec((1, n), lambda t, *_: (0, 0)),
                  pl.BlockSpec(memory_space=pl.ANY)],
        out_specs=pl.BlockSpec((ts, d), lambda t, *_: (t, 0)),
        scratch_shapes=[pltpu.VMEM((2, tc, d), BF16), pltpu.SemaphoreType.DMA((2,))],
    )
    return pl.pallas_call(
        _dispatch_kernel,
        out_shape=jax.ShapeDtypeStruct((n_slots, d), BF16),
        grid_spec=grid_spec,
        compiler_params=_params("arbitrary"),
        name="moe_dispatch",
    )(*tables, pos1, pos2, h)


def _expert_ffn_kernel(texp_ref, nused_ref, xs_ref, wg_ref, wu_ref, wd_ref, o_ref, acc_ref):
    i = pl.program_id(0)
    f = pl.program_id(1)

    @pl.when(f == 0)
    def _():
        acc_ref[...] = jnp.zeros_like(acc_ref)

    @pl.when(i < nused_ref[0])
    def _():
        xs = xs_ref[...]
        gate = jnp.dot(xs, wg_ref[0], preferred_element_type=F32)
        up = jnp.dot(xs, wu_ref[0], preferred_element_type=F32)
        act = (_silu(gate) * up).astype(BF16)
        acc_ref[...] += jnp.dot(act, wd_ref[0], preferred_element_type=F32)

    @pl.when(f == pl.num_programs(1) - 1)
    def _():
        o_ref[...] = acc_ref[...].astype(o_ref.dtype)


def _expert_ffn(xs, tile_expert, n_used, wg, wu, wd, ts, tf):
    n_slots, d = xs.shape
    ff = wg.shape[2]

    def fchunk(i, f, nu):
        return jnp.where(i < nu[0], f, 0)

    grid_spec = pltpu.PrefetchScalarGridSpec(
        num_scalar_prefetch=2,
        grid=(n_slots // ts, ff // tf),
        in_specs=[pl.BlockSpec((ts, d), lambda i, f, te, nu: (i, 0)),
                  pl.BlockSpec((1, d, tf), lambda i, f, te, nu: (te[i], 0, fchunk(i, f, nu))),
                  pl.BlockSpec((1, d, tf), lambda i, f, te, nu: (te[i], 0, fchunk(i, f, nu))),
                  pl.BlockSpec((1, tf, d), lambda i, f, te, nu: (te[i], fchunk(i, f, nu), 0))],
        out_specs=pl.BlockSpec((ts, d), lambda i, f, te, nu: (i, 0)),
        scratch_shapes=[pltpu.VMEM((ts, d), F32)],
    )
    return pl.pallas_call(
        _expert_ffn_kernel,
        out_shape=jax.ShapeDtypeStruct((n_slots, d), BF16),
        grid_spec=grid_spec,
        compiler_params=_params("arbitrary", "arbitrary"),
        name="moe_expert_ffn",
    )(tile_expert, n_used, xs, wg, wu, wd)


def _combine_kernel(seg_row_ref, seg_blocks_ref, seg_first_ref, seg_end_ref,
                    route_ref, gstart_ref, ys_hbm, x_ref, gf_ref, o_ref,
                    acc_ref, cols_ref, ybuf_ref, sem_ref):
    c = pl.program_id(0)
    n_chunks = pl.num_programs(0)
    tc = x_ref.shape[0]

    def block_copy(chunk, buf, e, k, blk):
        row0 = pl.multiple_of(seg_row_ref[chunk * N_EXPERTS + e] + k * COMBINE_ROWS, ROW_ALIGN)
        return pltpu.make_async_copy(ys_hbm.at[pl.ds(row0, COMBINE_ROWS), :], ybuf_ref.at[buf, blk],
                                     sem_ref.at[buf])

    def for_each_block(chunk, fn):
        blk = 0
        for e in range(N_EXPERTS):
            n_blk = seg_blocks_ref[chunk * N_EXPERTS + e]
            for k in range(COMBINE_MAX_BLOCKS):
                pl.when(k < n_blk)(functools.partial(fn, e, k, blk + k))
            blk = blk + n_blk

    def start_reads(chunk, buf):
        for_each_block(chunk, lambda e, k, blk: block_copy(chunk, buf, e, k, blk).start())

    @pl.when(c == 0)
    def _():
        start_reads(0, 0)

    @pl.when(c + 1 < n_chunks)
    def _():
        start_reads(c + 1, (c + 1) % 2)

    buf = c % 2
    acc_ref[...] = x_ref[...]
    route = route_ref[...]
    lane = lax.broadcasted_iota(I32, (1, LANES), 1).astype(F32)
    for which in range(TOP_K):
        start = jnp.sum(jnp.where(lane == route[:, which:which + 1], gstart_ref[...], 0.0),
                        axis=1, keepdims=True)
        pos = start + route[:, 4 + which:5 + which]
        cols_ref[which] = jnp.broadcast_to(pos, (tc, LANES))
        cols_ref[TOP_K + which] = jnp.broadcast_to(route[:, 2 + which:3 + which], (tc, LANES))

    rep = COMBINE_ROWS // LANES
    wide = lambda idx: jnp.concatenate([cols_ref[idx]] * rep, axis=1)

    for_each_block(c, lambda e, k, blk: block_copy(c, buf, e, k, blk).wait())

    def gather_block(e, k, blk):
        row0 = seg_row_ref[c * N_EXPERTS + e] + k * COMBINE_ROWS
        slot = row0 + lax.broadcasted_iota(I32, (1, COMBINE_ROWS), 1)
        inside = (slot >= seg_first_ref[c * N_EXPERTS + e]) & (slot < seg_end_ref[c * N_EXPERTS + e])
        slot = jnp.where(inside, slot, -1).astype(F32)
        gmat = jnp.where(slot == wide(0), wide(2), jnp.where(slot == wide(1), wide(3), 0.0))
        acc_ref[...] += jnp.dot(gmat.astype(BF16), ybuf_ref[buf, blk], preferred_element_type=F32)

    for_each_block(c, gather_block)
    o_ref[...] = _rms(acc_ref[...], gf_ref[...])


def _combine(x2, ys, route, gstart_row, g_final, segs, tc):
    n, d = x2.shape
    max_blocks = (TOP_K * tc + N_EXPERTS * (ROW_ALIGN - 1)) // COMBINE_ROWS + N_EXPERTS
    grid_spec = pltpu.PrefetchScalarGridSpec(
        num_scalar_prefetch=len(segs),
        grid=(n // tc,),
        in_specs=[pl.BlockSpec((tc, LANES), lambda c, *_: (c, 0)),
                  pl.BlockSpec((1, LANES), lambda c, *_: (0, 0)),
                  pl.BlockSpec(memory_space=pl.ANY),
                  pl.BlockSpec((tc, d), lambda c, *_: (c, 0)),
                  pl.BlockSpec((1, d), lambda c, *_: (0, 0))],
        out_specs=pl.BlockSpec((tc, d), lambda c, *_: (c, 0)),
        scratch_shapes=[pltpu.VMEM((tc, d), F32), pltpu.VMEM((2 * TOP_K, tc, LANES), F32),
                        pltpu.VMEM((2, max_blocks, COMBINE_ROWS, d), BF16),
                        pltpu.SemaphoreType.DMA((2,))],
    )
    return pl.pallas_call(
        _combine_kernel,
        out_shape=jax.ShapeDtypeStruct((n, d), F32),
        grid_spec=grid_spec,
        compiler_params=_params("arbitrary"),
        name="moe_combine",
    )(*segs, route, gstart_row, ys, x2, g_final)


def _count_le(ends, v):
    return jnp.sum((ends[None, :] <= v[:, None]).astype(I32), axis=1)


def _moe_top2(x2, g, w_router, wg, wu, wd, g_final):
    n, d = x2.shape
    ts, tc = T_SLOT, min(T_CHUNK, n)
    n_chunks = n // tc
    n_tiles = TOP_K * n // ts + N_EXPERTS + 1
    n_slots = n_tiles * ts

    h, route, cstart, ctotal = _router(x2, g, w_router, tc)

    experts = jnp.arange(N_EXPERTS, dtype=I32)
    counts = ctotal[0, :N_EXPERTS].astype(I32)
    padded = (counts + ts - 1) // ts * ts
    group_end = jnp.cumsum(padded)
    group_start = group_end - padded
    n_used = (group_end[-1] // ts).astype(I32)
    tiles = jnp.arange(n_tiles, dtype=I32)
    tile_expert = jnp.minimum(_count_le(group_end // ts, tiles), N_EXPERTS - 1)
    c_before = cstart[:, 0, :N_EXPERTS].astype(I32)
    c_after = jnp.concatenate([c_before[1:], counts[None, :]], axis=0)

    e1, e2 = route[:, 0].astype(I32), route[:, 1].astype(I32)
    start_of = lambda e: jnp.sum(jnp.where(e[:, None] == experts, group_start, 0), axis=1)
    pos1 = start_of(e1) + route[:, 4].astype(I32)
    pos2 = start_of(e2) + route[:, 5].astype(I32)

    rank_lo = tiles * ts - group_start[tile_expert]
    rank_hi = jnp.minimum(rank_lo + ts, counts[tile_expert]) - 1
    after_t = c_after[:, tile_expert]
    used = tiles < n_used
    c_lo = jnp.where(used, jnp.sum((after_t <= rank_lo).astype(I32), axis=0), 0)
    c_hi = jnp.sum((after_t <= rank_hi).astype(I32), axis=0)
    d_tables = (tile_expert, c_lo, jnp.where(used, c_hi - c_lo + 1, 0), group_start,
                c_before.reshape(-1), c_after.reshape(-1))

    seg_first = group_start + c_before
    seg_row = seg_first // ROW_ALIGN * ROW_ALIGN
    seg_blocks = jnp.where(c_after > c_before,
                           (seg_first - seg_row + c_after - c_before + COMBINE_ROWS - 1) // COMBINE_ROWS, 0)

    xs = _dispatch(h, pos1.reshape(1, n), pos2.reshape(1, n), d_tables, n_slots, ts, tc)
    ys = _expert_ffn(xs, tile_expert, n_used.reshape(1), wg, wu, wd, ts, min(TF_EXPERT, wg.shape[2]))
    gstart_row = _pad_cols(group_start.astype(F32).reshape(1, N_EXPERTS), LANES)
    segs = tuple(t.reshape(-1) for t in (seg_row, seg_blocks, seg_first, seg_first + c_after - c_before))
    return _combine(x2, ys, route, gstart_row, g_final, segs, tc)


def _pad_cols(w, width):
    return jnp.pad(w, ((0, 0), (0, width - w.shape[1])))


def _prep_w_in(w):
    parts = []
    off = 0
    for size in IN_SIZES:
        parts.append(w[:, off:off + size])
        off += size
    cq, ckv, kr, q_b, k_b, v_b, q_c, k_c, v_c, f_c, q_d, k_d, v_d = parts
    d = w.shape[0]
    kr_blk = jnp.concatenate([jnp.zeros((d, MLA_NOPE), w.dtype), kr,
                              jnp.zeros((d, LANES - MLA_NOPE - MLA_ROPE), w.dtype)], axis=1)
    cols = [cq, ckv, kr_blk, _pad_cols(f_c, LANES), q_b, k_b, v_b, q_c, k_c, v_c, q_d, k_d, v_d]
    return jnp.concatenate(cols, axis=1).astype(BF16)


def _prep_mla(w_uq, w_ukv):
    qk = MLA_NOPE + MLA_ROPE
    wuq = jnp.pad(w_uq.reshape(MLA_Q_RANK, GROUP_HEADS, qk), ((0, 0), (0, 0), (0, LANES - qk)))
    wuq = wuq.reshape(MLA_Q_RANK, GROUP_HEADS * LANES)
    kv = w_ukv.reshape(MLA_KV_RANK, GROUP_HEADS, MLA_NOPE + HEAD_DIM)
    wuk = jnp.pad(kv[:, :, :MLA_NOPE], ((0, 0), (0, 0), (0, LANES - MLA_NOPE)))
    wuk = wuk.reshape(MLA_KV_RANK, GROUP_HEADS * LANES)
    wuv = kv[:, :, MLA_NOPE:].reshape(MLA_KV_RANK, GROUP_WIDTH)
    return wuq.astype(BF16), wuk.astype(BF16), wuv.astype(BF16)


def _row(v, width=None):
    v = v.reshape(1, -1).astype(F32)
    return v if width is None else _pad_cols(v, width)


def kernel(x, g_mix, w_in, b_forget, g_q_lat, g_kv_lat, w_uq, w_ukv, lambda_q1, lambda_k1, lambda_q2,
           lambda_k2, g_diff_sub, w_out, g_ffn, w_ffn_gate, w_ffn_up, w_ffn_down, w_router, w_exp_gate,
           w_exp_up, w_exp_down, g_final):
    b, s_len, d = x.shape
    depth = g_mix.shape[0]
    assert d == D_MODEL and depth == 2, "kernel is specialised to the two-layer trunk"
    assert w_router.shape[-1] == N_EXPERTS
    n = b * s_len
    t_attn = T_ATTN
    tm_in = min(TM_PROJ, s_len)
    tm_tok = min(TM_TOK, n)
    assert s_len % t_attn == 0 and s_len % tm_in == 0 and n % tm_tok == 0
    assert n % min(T_CHUNK, n) == 0 and (TOP_K * n) % T_SLOT == 0

    tabs = _rope_tables(s_len)
    log_mult = _dilated_log2_multiplicity(t_attn)
    n_lm = log_mult.shape[0]

    for l in range(depth):
        wuq, wuk, wuv = _prep_mla(w_uq[l], w_ukv[l])
        zz, vt, zf = _proj_in(x, _row(g_mix[l]), _prep_w_in(w_in[l]), _row(g_q_lat[l]),
                              _row(g_kv_lat[l]), wuq, wuk, wuv, tabs, tm_in)
        crep, crow = _forget_cumsum(zf, _row(b_forget[l], LANES), tm_in)

        lam_init = 0.8 - 0.6 * math.exp(-0.3 * l)
        lamv = jnp.concatenate(
            [_row(v, LANES) for v in (lambda_q1[l], lambda_k1[l], lambda_q2[l], lambda_k2[l])]
            + [jnp.zeros((4, LANES), F32)], axis=0)
        gsub = jnp.tile(_row(g_diff_sub[l]), (1, GROUP_HEADS))

        y_a = _attention(zz, vt, 0, "mla", t_attn, ZZ_QA, ZZ_KA)
        c0 = ZZ_QB
        y_b = _attention(zz, vt, 1, "dil", t_attn, c0, c0 + GROUP_WIDTH, extra=(log_mult,),
                         extra_specs=(pl.BlockSpec((n_lm, t_attn, t_attn), lambda bi, qi: (0, 0, 0)),))
        c0 += 2 * GROUP_WIDTH
        y_c = _attention(zz, vt, 2, "fox", t_attn, c0, c0 + GROUP_WIDTH, extra=(crep, crow),
                         extra_specs=(pl.BlockSpec((1, GROUP_HEADS, s_len, LANES), lambda bi, qi: (bi, 0, 0, 0)),
                                      pl.BlockSpec((1, 8, s_len), lambda bi, qi: (bi, 0, 0))))
        c0 += 2 * GROUP_WIDTH
        y_d = _attention(zz, vt, 3, "diff", t_attn, c0, c0 + GROUP_WIDTH, extra=(lamv, gsub),
                         extra_specs=(pl.BlockSpec((8, LANES), lambda bi, qi: (0, 0)),
                                      pl.BlockSpec((1, GROUP_WIDTH), lambda bi, qi: (0, 0))),
                         lam_init=lam_init)

        ys = [y.reshape(n, GROUP_WIDTH) for y in (y_a, y_b, y_c, y_d)]
        x2 = _proj_out(x.reshape(n, d), ys, w_out[l].astype(BF16), tm_tok)

        i = l // 2
        if l % 2 == 0:
            x2 = _ffn(x2, _row(g_ffn[l]), w_ffn_gate[i].astype(BF16), w_ffn_up[i].astype(BF16),
                      w_ffn_down[i].astype(BF16), tm_tok, TF_DENSE)
        else:
            x2 = _moe_top2(x2, _row(g_ffn[l]), _pad_cols(w_router[i].astype(F32), LANES),
                           w_exp_gate[i].astype(BF16), w_exp_up[i].astype(BF16),
                           w_exp_down[i].astype(BF16), _row(g_final))
        x = x2.reshape(b, s_len, d)
    return x
```

```python
import functools
import math

import jax
import jax.numpy as jnp
from jax import lax
from jax.experimental import pallas as pl
from jax.experimental.pallas import tpu as pltpu

F32 = jnp.float32
BF16 = jnp.bfloat16
I32 = jnp.int32

D_MODEL = 1024
HEAD_DIM = 64
GROUP_HEADS = 4
GROUP_WIDTH = GROUP_HEADS * HEAD_DIM
N_MIXERS = 4
MLA_Q_RANK = 256
MLA_KV_RANK = 128
MLA_NOPE = 64
MLA_ROPE = 32
DIL_CONFIGS = ((128, 1), (512, 4), (2048, 16))
DIL_MAX_WINDOW = max(w for w, _ in DIL_CONFIGS)
DIFF_QK = HEAD_DIM // 2
N_EXPERTS = 8
TOP_K = 2
ROPE_THETA = 10000.0
NORM_EPS = 1e-6
NEG_INF = -1e30
LOG2E = math.log2(math.e)
ACC_ROWS = HEAD_DIM + 16
LANES = 128
IN_SIZES = (MLA_Q_RANK, MLA_KV_RANK, MLA_ROPE,
            GROUP_WIDTH, GROUP_WIDTH, GROUP_WIDTH,
            GROUP_WIDTH, GROUP_WIDTH, GROUP_WIDTH, GROUP_HEADS,
            GROUP_WIDTH, GROUP_WIDTH, GROUP_WIDTH)

WP_CQ, WP_CKV, WP_KR, WP_F = 0, 256, 384, 512
WP_QB = 640
WP_WIDTH = WP_QB + 9 * GROUP_WIDTH

ZZ_QA, ZZ_KA = 0, 512
ZZ_QB = 1024
ZZ_WIDTH = ZZ_QB + 6 * GROUP_WIDTH

ROPE_A, ROPE_B, ROPE_D = 0, 1, 2

VMEM_LIMIT = 56 * 1024 * 1024

T_ATTN = 512
TM_PROJ = 512
TM_TOK = 1024
TF_DENSE = 256
T_SLOT = 512
T_CHUNK = 512
TF_EXPERT = 1792
SUB_DISPATCH = 128
ROW_ALIGN = 16
COMBINE_ROWS = 256
COMBINE_MAX_BLOCKS = -(-(T_CHUNK + ROW_ALIGN - 1) // COMBINE_ROWS)


def _params(*sem):
    return pltpu.CompilerParams(dimension_semantics=sem, vmem_limit_bytes=VMEM_LIMIT)


def _rope_tables(s_len):
    pos = jnp.arange(s_len, dtype=F32)[:, None]
    lane = jnp.arange(LANES)[None, :]

    def kind(rel, width, active):
        half = width // 2
        idx = (rel % half).astype(F32)
        inv = ROPE_THETA ** (-idx / half)
        ang = pos * inv
        cos, sin = jnp.cos(ang), jnp.sin(ang)
        first = active & (rel < half)
        second = active & (rel >= half)
        c = jnp.where(active, cos, 1.0)
        sm = jnp.where(first, -sin, 0.0)
        sp = jnp.where(second, sin, 0.0)
        return [c, sm, sp]

    a_active = (lane >= MLA_NOPE) & (lane < MLA_NOPE + MLA_ROPE)
    tabs = (kind(jnp.where(a_active, lane - MLA_NOPE, 0), MLA_ROPE, a_active)
            + kind(lane % HEAD_DIM, HEAD_DIM, lane >= 0)
            + kind(lane % DIFF_QK, DIFF_QK, lane >= 0))
    return jnp.stack([jnp.broadcast_to(t, (s_len, LANES)).astype(F32) for t in tabs])


_ROPE_HALF = {ROPE_A: MLA_ROPE // 2, ROPE_B: HEAD_DIM // 2, ROPE_D: DIFF_QK // 2}


def _rope(blk, tab_ref, kind):
    half = _ROPE_HALF[kind]
    c = tab_ref[3 * kind]
    sm = tab_ref[3 * kind + 1]
    sp = tab_ref[3 * kind + 2]
    return blk * c + pltpu.roll(blk, LANES - half, 1) * sm + pltpu.roll(blk, half, 1) * sp


def _rms(x, g):
    return x * lax.rsqrt(jnp.mean(x * x, axis=-1, keepdims=True) + NORM_EPS) * g


def _proj_in_kernel(x_ref, g_ref, w_ref, gql_ref, gkvl_ref, wuq_ref, wuk_ref, wuv_ref, tab_ref,
                    zz_ref, vt_ref, zf_ref):
    h = _rms(x_ref[0], g_ref[...]).astype(BF16)

    def proj(c0, width):
        return jnp.dot(h, w_ref[:, c0:c0 + width], preferred_element_type=F32)

    def put(col, val):
        zz_ref[0, :, col:col + LANES] = val.astype(BF16)

    def put_v(mixer, val):
        vt_ref[0, mixer * GROUP_WIDTH:(mixer + 1) * GROUP_WIDTH, :] = val.T.astype(BF16)

    latent = {}
    scale_a = (MLA_NOPE + MLA_ROPE) ** -0.5 * LOG2E
    plan = (((ROPE_B, HEAD_DIM ** -0.5 * LOG2E), (ROPE_B, 1.0)),
            ((None, HEAD_DIM ** -0.5 * LOG2E), (None, 1.0)),
            ((ROPE_D, DIFF_QK ** -0.5 * LOG2E), (ROPE_D, 1.0)))

    def group(idx):
        return lambda: proj(WP_QB + idx * GROUP_WIDTH, GROUP_WIDTH)

    def qk_epilogue(mx, t, kind, scale, z):
        for half in range(GROUP_WIDTH // LANES):
            blk = z[:, half * LANES:(half + 1) * LANES]
            if kind is not None:
                blk = _rope(blk, tab_ref, kind)
            if scale != 1.0:
                blk = blk * scale
            put(ZZ_QB + (2 * mx + t) * GROUP_WIDTH + half * LANES, blk)

    def mla_epilogue(col, scale, z):
        for hh in range(GROUP_HEADS):
            blk = _rope(z[:, hh * LANES:(hh + 1) * LANES], tab_ref, ROPE_A)
            put(col + hh * LANES, blk * scale if scale != 1.0 else blk)

    def keep(name, g_ref_, z):
        latent[name] = _rms(z, g_ref_[...]).astype(BF16)

    def store_f(z):
        zf_ref[0] = z

    def mixer_tasks(mx):
        out = [(group(3 * mx + t), functools.partial(qk_epilogue, mx, t, kind, scale))
               for t, (kind, scale) in enumerate(plan[mx])]
        return out + [(group(3 * mx + 2), functools.partial(put_v, mx + 1))]

    tasks = [
        (lambda: proj(WP_CQ, MLA_Q_RANK), functools.partial(keep, "cq", gql_ref)),
        (lambda: proj(WP_CKV, MLA_KV_RANK), functools.partial(keep, "ckv", gkvl_ref)),
        *mixer_tasks(0)[:2],
        (lambda: jnp.dot(latent["cq"], wuq_ref[...], preferred_element_type=F32),
         functools.partial(mla_epilogue, ZZ_QA, scale_a)),
        (lambda: jnp.dot(latent["ckv"], wuk_ref[...], preferred_element_type=F32)
         + jnp.concatenate([proj(WP_KR, LANES)] * GROUP_HEADS, axis=1),
         functools.partial(mla_epilogue, ZZ_KA, 1.0)),
        (lambda: jnp.dot(latent["ckv"], wuv_ref[...], preferred_element_type=F32), functools.partial(put_v, 0)),
        (lambda: proj(WP_F, LANES), store_f),
        *mixer_tasks(0)[2:], *mixer_tasks(1), *mixer_tasks(2),
    ]
    z = tasks[0][0]()
    for i, (_, epilogue) in enumerate(tasks):
        z_next = tasks[i + 1][0]() if i + 1 < len(tasks) else None
        epilogue(z)
        z = z_next


def _proj_in(x, g, w, gql, gkvl, wuq, wuk, wuv, tabs, tm):
    b, s_len, d = x.shape
    const = lambda shape: pl.BlockSpec(shape, lambda bi, si: (0,) * len(shape))
    return pl.pallas_call(
        _proj_in_kernel,
        out_shape=(jax.ShapeDtypeStruct((b, s_len, ZZ_WIDTH), BF16),
                   jax.ShapeDtypeStruct((b, N_MIXERS * GROUP_WIDTH, s_len), BF16),
                   jax.ShapeDtypeStruct((b, s_len, LANES), F32)),
        grid=(b, s_len // tm),
        in_specs=[
            pl.BlockSpec((1, tm, d), lambda bi, si: (bi, si, 0)),
            const((1, d)), const((d, WP_WIDTH)),
            const((1, MLA_Q_RANK)), const((1, MLA_KV_RANK)),
            const((MLA_Q_RANK, GROUP_HEADS * LANES)),
            const((MLA_KV_RANK, GROUP_HEADS * LANES)),
            const((MLA_KV_RANK, GROUP_WIDTH)),
            pl.BlockSpec((9, tm, LANES), lambda bi, si: (0, si, 0)),
        ],
        out_specs=(pl.BlockSpec((1, tm, ZZ_WIDTH), lambda bi, si: (bi, si, 0)),
                   pl.BlockSpec((1, N_MIXERS * GROUP_WIDTH, tm), lambda bi, si: (bi, 0, si)),
                   pl.BlockSpec((1, tm, LANES), lambda bi, si: (bi, si, 0))),
        compiler_params=_params("parallel", "parallel"),
        name="proj_in",
    )(x, g, w, gql, gkvl, wuq, wuk, wuv, tabs)


def _forget_cumsum_kernel(zf_ref, bias_ref, crep_ref, crow_ref, carry_ref):
    @pl.when(pl.program_id(1) == 0)
    def _():
        carry_ref[...] = jnp.zeros_like(carry_ref)

    t = zf_ref[0] + bias_ref[...]
    log_f = jnp.minimum(t, 0.0) - jnp.log(1.0 + jnp.exp(-jnp.abs(t)))
    tc = log_f.shape[0]
    row = lax.broadcasted_iota(I32, (tc, tc), 0)
    col = lax.broadcasted_iota(I32, (tc, tc), 1)
    tri = jnp.where(col <= row, 1.0, 0.0).astype(F32)
    cs = jnp.dot(tri, log_f, precision=lax.Precision.HIGHEST, preferred_element_type=F32) + carry_ref[...]
    carry_ref[...] = cs[tc - 1:tc, :]
    cs2 = cs * LOG2E
    crow_ref[0] = cs2.T[0:8, :]
    for hh in range(GROUP_HEADS):
        crep_ref[0, hh] = jnp.broadcast_to(cs2[:, hh:hh + 1], (tc, LANES))


def _forget_cumsum(zf, bias, tc):
    b, s_len, _ = zf.shape
    return pl.pallas_call(
        _forget_cumsum_kernel,
        out_shape=(jax.ShapeDtypeStruct((b, GROUP_HEADS, s_len, LANES), F32),
                   jax.ShapeDtypeStruct((b, 8, s_len), F32)),
        grid=(b, s_len // tc),
        in_specs=[pl.BlockSpec((1, tc, LANES), lambda bi, si: (bi, si, 0)),
                  pl.BlockSpec((1, LANES), lambda bi, si: (0, 0))],
        out_specs=(pl.BlockSpec((1, GROUP_HEADS, tc, LANES), lambda bi, si: (bi, 0, si, 0)),
                   pl.BlockSpec((1, 8, tc), lambda bi, si: (bi, 0, si))),
        scratch_shapes=[pltpu.VMEM((1, LANES), F32)],
        compiler_params=_params("parallel", "arbitrary"),
        name="forget_cumsum",
    )(zf, bias)


def _dilated_log2_multiplicity(t):
    n_tiles = DIL_MAX_WINDOW // t + 1
    k = jnp.arange(n_tiles)[:, None, None]
    j = jnp.arange(t)[None, :, None]
    i = jnp.arange(t)[None, None, :]
    delta = k * t + i - j
    mult = jnp.zeros(delta.shape, F32)
    for window, dilation in DIL_CONFIGS:
        mult = mult + ((delta >= 0) & (delta <= window) & (delta % dilation == 0)).astype(F32)
    return jnp.where(mult > 0, jnp.log2(jnp.maximum(mult, 1.0)), NEG_INF).astype(F32)


def _attn_kernel(*refs, mixer, t, lam_init):
    q_ref, k_ref, vt_ref = refs[:3]
    o_ref, acc_ref, sa_ref, sb_ref = refs[-4:]
    extra = refs[3:-4]
    qi = pl.program_id(1)
    sep = mixer == "mla"
    nsub = 2 * GROUP_HEADS if mixer == "diff" else GROUP_HEADS
    per_grp = 1 if sep else nsub // 2
    sub_w = LANES // per_grp

    lane = lax.broadcasted_iota(I32, (1, LANES), 1)
    q = q_ref[0]
    qs = []
    for hh in range(nsub):
        grp, pos = hh // per_grp, hh % per_grp
        qg = q[:, grp * LANES:(grp + 1) * LANES]
        if not sep:
            qg = jnp.where((lane >= pos * sub_w) & (lane < (pos + 1) * sub_w), qg, jnp.zeros_like(qg))
        qs.append(qg)

    if mixer == "fox":
        crep_ref, crow_ref = extra
        q_off = pl.multiple_of(qi * t, t)
        cq = [crow_ref[0, hh:hh + 1, pl.ds(q_off, t)] for hh in range(nsub)]
    if mixer == "dil":
        (lm_ref,) = extra

    key_idx = lax.broadcasted_iota(I32, (t, t), 0)
    qry_idx = lax.broadcasted_iota(I32, (t, t), 1)
    causal = key_idx <= qry_idx

    acc_ref[...] = jnp.zeros_like(acc_ref)

    def scores(j, s_ref):
        off = pl.multiple_of(j * t, t)
        k = k_ref[0, pl.ds(off, t), :]
        for hh in range(nsub):
            grp = hh // per_grp
            s_ref[hh] = lax.dot_general(k[:, grp * LANES:(grp + 1) * LANES], qs[hh],
                                        (((1,), (1,)), ((), ())), preferred_element_type=F32)

    def consume(j, s_ref, ms, diag):
        off = pl.multiple_of(j * t, t)
        ss = []
        for hh in range(nsub):
            s = s_ref[hh]
            if mixer == "fox":
                ck = crep_ref[0, hh, pl.ds(off, t), :]
                s = s - jnp.concatenate([ck] * (t // LANES), axis=1)
            if mixer == "dil":
                s = s + lm_ref[qi - j]
            elif diag:
                s = jnp.where(causal, s, NEG_INF)
            ss.append(s)
        if mixer == "fox":
            new_m = [jnp.maximum(ms[hh], jnp.max(ss[hh], axis=0, keepdims=True) + cq[hh]) for hh in range(nsub)]
            ps = [jnp.exp2(ss[hh] - (new_m[hh] - cq[hh])).astype(BF16) for hh in range(nsub)]
        else:
            new_m = [jnp.maximum(ms[hh], jnp.max(ss[hh], axis=0, keepdims=True)) for hh in range(nsub)]
            ps = [jnp.exp2(ss[hh] - new_m[hh]).astype(BF16) for hh in range(nsub)]
        alphas = [jnp.exp2(ms[hh] - new_m[hh]) for hh in range(nsub)]
        pvs = []
        for hh in range(nsub):
            vh = hh * GROUP_HEADS // nsub
            vt = vt_ref[0, vh * HEAD_DIM:(vh + 1) * HEAD_DIM, pl.ds(off, t)]
            vt1 = jnp.concatenate([vt, jnp.ones((ACC_ROWS - HEAD_DIM, t), BF16)], axis=0)
            pvs.append(jnp.dot(vt1, ps[hh], preferred_element_type=F32))
        for hh in range(nsub):
            acc_ref[hh] = acc_ref[hh] * alphas[hh] + pvs[hh]
        return tuple(new_m)

    def run_tiles(j, count, ms):
        for u in range(0, count, 2):
            scores(j + u + 1, sb_ref)
            ms = consume(j + u, sa_ref, ms, False)
            scores(j + u + 2, sa_ref)
            ms = consume(j + u + 1, sb_ref, ms, False)
        return ms

    lo = jnp.maximum(qi - DIL_MAX_WINDOW // t, 0) if mixer == "dil" else 0
    n_full = qi - lo
    init = tuple(jnp.full((1, t), NEG_INF, F32) for _ in range(nsub))
    scores(lo, sa_ref)
    n_quads = n_full // 4
    ms = lax.fori_loop(0, n_quads, lambda i, ms: run_tiles(lo + 4 * i, 4, ms), init)
    rest = lo + 4 * n_quads
    ms = lax.fori_loop(0, (n_full % 4) // 2, lambda i, ms: run_tiles(rest, 2, ms), ms)

    def odd_tail(ms):
        scores(qi, sb_ref)
        ms = consume(qi - 1, sa_ref, ms, False)
        return consume(qi, sb_ref, ms, True)

    def even_tail(ms):
        return consume(qi, sa_ref, ms, True)

    lax.cond(n_full % 2 == 1, odd_tail, even_tail, ms)

    def head_out(hh):
        a = acc_ref[hh]
        return a[:HEAD_DIM] / a[HEAD_DIM:HEAD_DIM + 1]

    if mixer != "diff":
        o_t = jnp.concatenate([head_out(hh) for hh in range(nsub)], axis=0)
        o_ref[0] = o_t.T.astype(o_ref.dtype)
    else:
        lamv_ref, gsub_ref = extra
        lamv = lamv_ref[...]
        e1 = jnp.exp(jnp.sum(lamv[0:1] * lamv[1:2], axis=1, keepdims=True))
        e2 = jnp.exp(jnp.sum(lamv[2:3] * lamv[3:4], axis=1, keepdims=True))
        lam = e1 - e2 + lam_init
        heads = []
        for vh in range(GROUP_HEADS):
            o = head_out(2 * vh) - lam * head_out(2 * vh + 1)
            ms_h = jnp.mean(o * o, axis=0, keepdims=True)
            heads.append(o * lax.rsqrt(ms_h + NORM_EPS))
        o_t = jnp.concatenate(heads, axis=0)
        o_ref[0] = (o_t.T * gsub_ref[...] * (1.0 - lam_init)).astype(o_ref.dtype)


def _attention(zz, vt, mixer_idx, mixer, t, q_col, k_col, extra=(), extra_specs=(), lam_init=0.0):
    b, s_len, _ = zz.shape
    qw = GROUP_HEADS * LANES if mixer == "mla" else GROUP_WIDTH
    qb, kb = q_col // qw, k_col // qw
    nsub = 2 * GROUP_HEADS if mixer == "diff" else GROUP_HEADS
    return pl.pallas_call(
        functools.partial(_attn_kernel, mixer=mixer, t=t, lam_init=lam_init),
        out_shape=jax.ShapeDtypeStruct((b, s_len, GROUP_WIDTH), BF16),
        grid=(b, s_len // t),
        in_specs=[pl.BlockSpec((1, t, qw), lambda bi, qi: (bi, qi, qb)),
                  pl.BlockSpec((1, s_len, qw), lambda bi, qi: (bi, 0, kb)),
                  pl.BlockSpec((1, GROUP_WIDTH, s_len), lambda bi, qi: (bi, mixer_idx, 0)),
                  *extra_specs],
        out_specs=pl.BlockSpec((1, t, GROUP_WIDTH), lambda bi, qi: (bi, qi, 0)),
        scratch_shapes=[pltpu.VMEM((nsub, ACC_ROWS, t), F32), pltpu.VMEM((nsub, t, t), F32),
                        pltpu.VMEM((nsub, t, t), F32)],
        compiler_params=_params("parallel", "arbitrary"),
        name="attn_" + mixer,
    )(zz, zz, vt, *extra)


def _proj_out_kernel(x_ref, ya_ref, yb_ref, yc_ref, yd_ref, w_ref, o_ref):
    acc = x_ref[...]
    for i, y_ref in enumerate((ya_ref, yb_ref, yc_ref, yd_ref)):
        acc = acc + jnp.dot(y_ref[...], w_ref[i * GROUP_WIDTH:(i + 1) * GROUP_WIDTH, :],
                            preferred_element_type=F32)
    o_ref[...] = acc


def _proj_out(x2, ys, w, tm):
    n, d = x2.shape
    yspec = pl.BlockSpec((tm, GROUP_WIDTH), lambda i: (i, 0))
    return pl.pallas_call(
        _proj_out_kernel,
        out_shape=jax.ShapeDtypeStruct((n, d), F32),
        grid=(n // tm,),
        in_specs=[pl.BlockSpec((tm, d), lambda i: (i, 0)), yspec, yspec, yspec, yspec,
                  pl.BlockSpec((4 * GROUP_WIDTH, d), lambda i: (0, 0))],
        out_specs=pl.BlockSpec((tm, d), lambda i: (i, 0)),
        compiler_params=_params("parallel"),
        name="proj_out",
    )(x2, *ys, w)


def _silu(gate):
    return gate / (1.0 + jnp.exp(-gate))


def _ffn_kernel(x_ref, g_ref, wg_ref, wu_ref, wd_ref, o_ref, h_ref, acc_ref):
    f = pl.program_id(1)

    @pl.when(f == 0)
    def _():
        x = x_ref[...]
        h_ref[...] = _rms(x, g_ref[...]).astype(BF16)
        acc_ref[...] = x

    h = h_ref[...]
    gate = jnp.dot(h, wg_ref[...], preferred_element_type=F32)
    up = jnp.dot(h, wu_ref[...], preferred_element_type=F32)
    act = (_silu(gate) * up).astype(BF16)
    acc_ref[...] += jnp.dot(act, wd_ref[...], preferred_element_type=F32)

    @pl.when(f == pl.num_programs(1) - 1)
    def _():
        o_ref[...] = acc_ref[...]


def _ffn(x2, g, wg, wu, wd, tm, tf):
    n, d = x2.shape
    ff = wg.shape[1]
    return pl.pallas_call(
        _ffn_kernel,
        out_shape=jax.ShapeDtypeStruct((n, d), F32),
        grid=(n // tm, ff // tf),
        in_specs=[pl.BlockSpec((tm, d), lambda i, f: (i, 0)),
                  pl.BlockSpec((1, d), lambda i, f: (0, 0)),
                  pl.BlockSpec((d, tf), lambda i, f: (0, f)),
                  pl.BlockSpec((d, tf), lambda i, f: (0, f)),
                  pl.BlockSpec((tf, d), lambda i, f: (f, 0))],
        out_specs=pl.BlockSpec((tm, d), lambda i, f: (i, 0)),
        scratch_shapes=[pltpu.VMEM((tm, d), BF16), pltpu.VMEM((tm, d), F32)],
        compiler_params=_params("parallel", "arbitrary"),
        name="ffn_dense",
    )(x2, g, wg, wu, wd)


def _router_kernel(x_ref, g_ref, wr_ref, h_ref, route_ref, cstart_ref, ctotal_ref, carry_ref):
    @pl.when(pl.program_id(0) == 0)
    def _():
        carry_ref[...] = jnp.zeros_like(carry_ref)

    h = _rms(x_ref[...], g_ref[...])
    h_ref[...] = h.astype(BF16)
    logits = jnp.dot(h, wr_ref[...], precision=lax.Precision.HIGHEST, preferred_element_type=F32)
    lane = lax.broadcasted_iota(I32, logits.shape, 1)
    lg = jnp.where(lane < N_EXPERTS, logits, -jnp.inf)
    m1 = jnp.max(lg, axis=1, keepdims=True)
    i1 = jnp.min(jnp.where(lg == m1, lane, LANES), axis=1, keepdims=True)
    lg2 = jnp.where(lane == i1, -jnp.inf, lg)
    m2 = jnp.max(lg2, axis=1, keepdims=True)
    i2 = jnp.min(jnp.where(lg2 == m2, lane, LANES), axis=1, keepdims=True)
    e = jnp.exp(m2 - m1)
    w1 = 1.0 / (1.0 + e)
    w2 = e / (1.0 + e)
    tm = lg.shape[0]
    chosen = jnp.where(lane == i1, 1.0, jnp.where(lane == i2, 1.0, 0.0))
    row = lax.broadcasted_iota(I32, (tm, tm), 0)
    col = lax.broadcasted_iota(I32, (tm, tm), 1)
    tri = jnp.where(col < row, 1.0, 0.0).astype(BF16)
    carry = carry_ref[...]
    before = jnp.dot(tri, chosen.astype(BF16), preferred_element_type=F32) + carry
    r1 = jnp.sum(jnp.where(lane == i1, before, 0.0), axis=1, keepdims=True)
    r2 = jnp.sum(jnp.where(lane == i2, before, 0.0), axis=1, keepdims=True)
    vals = (i1.astype(F32), i2.astype(F32), w1, w2, r1, r2)
    route = jnp.zeros_like(lg)
    for idx, val in enumerate(vals):
        route = jnp.where(lane == idx, val, route)
    route_ref[...] = route
    after = carry + jnp.sum(chosen, axis=0, keepdims=True)
    cstart_ref[0] = jnp.broadcast_to(carry, cstart_ref.shape[1:])
    ctotal_ref[...] = jnp.broadcast_to(after, ctotal_ref.shape)
    carry_ref[...] = after


def _router(x2, g, w_router, tm):
    n, d = x2.shape
    return pl.pallas_call(
        _router_kernel,
        out_shape=(jax.ShapeDtypeStruct((n, d), BF16), jax.ShapeDtypeStruct((n, LANES), F32),
                   jax.ShapeDtypeStruct((n // tm, 8, LANES), F32), jax.ShapeDtypeStruct((8, LANES), F32)),
        grid=(n // tm,),
        in_specs=[pl.BlockSpec((tm, d), lambda i: (i, 0)),
                  pl.BlockSpec((1, d), lambda i: (0, 0)),
                  pl.BlockSpec((d, LANES), lambda i: (0, 0))],
        out_specs=(pl.BlockSpec((tm, d), lambda i: (i, 0)), pl.BlockSpec((tm, LANES), lambda i: (i, 0)),
                   pl.BlockSpec((1, 8, LANES), lambda i: (i, 0, 0)), pl.BlockSpec((8, LANES), lambda i: (0, 0))),
        scratch_shapes=[pltpu.VMEM((1, LANES), F32)],
        compiler_params=_params("arbitrary"),
        name="moe_router",
    )(x2, g, w_router)


def _dispatch_kernel(texp_ref, clo_ref, nvis_ref, gstart_ref, cbefore_ref, cafter_ref,
                     pos1_ref, pos2_ref, h_hbm, o_ref, hbuf_ref, sem_ref):
    t = pl.program_id(0)
    ts, tc = o_ref.shape[0], hbuf_ref.shape[1]
    expert, first_chunk, n_vis = texp_ref[t], clo_ref[t], nvis_ref[t]

    def chunk_copy(i, buf):
        row0 = pl.multiple_of((first_chunk + i) * tc, tc)
        return pltpu.make_async_copy(h_hbm.at[pl.ds(row0, tc), :], hbuf_ref.at[buf], sem_ref.at[buf])

    o_ref[...] = jnp.zeros_like(o_ref)

    @pl.when(n_vis > 0)
    def _():
        chunk_copy(0, 0).start()

    def visit(i, carry):
        buf = i % 2

        @pl.when(i + 1 < n_vis)
        def _():
            chunk_copy(i + 1, 1 - buf).start()

        chunk_copy(i, buf).wait()
        chunk = first_chunk + i
        base = gstart_ref[expert] - t * ts
        lo = jnp.clip(base + cbefore_ref[chunk * N_EXPERTS + expert], 0, ts)
        hi = jnp.clip(base + cafter_ref[chunk * N_EXPERTS + expert], 0, ts)
        off = pl.multiple_of(chunk * tc, tc)
        pos1 = pos1_ref[:, pl.ds(off, tc)]
        pos2 = pos2_ref[:, pl.ds(off, tc)]
        for blk in range(ts // SUB_DISPATCH):
            r0 = blk * SUB_DISPATCH

            @pl.when((lo < r0 + SUB_DISPATCH) & (hi > r0))
            def _(r0=r0):
                slot = t * ts + r0 + lax.broadcasted_iota(I32, (SUB_DISPATCH, tc), 0)
                hit = jnp.where(slot == pos1, 1.0, jnp.where(slot == pos2, 1.0, 0.0)).astype(BF16)
                rows = jnp.dot(hit, hbuf_ref[buf], preferred_element_type=F32).astype(o_ref.dtype)
                o_ref[r0:r0 + SUB_DISPATCH, :] = o_ref[r0:r0 + SUB_DISPATCH, :] + rows

        return carry

    lax.fori_loop(0, n_vis, visit, 0)


def _dispatch(h, pos1, pos2, tables, n_slots, ts, tc):
    n, d = h.shape
    grid_spec = pltpu.PrefetchScalarGridSpec(
        num_scalar_prefetch=len(tables),
        grid=(n_slots // ts,),
        in_specs=[pl.BlockSpec((1, n), lambda t, *_: (0, 0)),
                  pl.BlockSpec((1, n), lambda t, *_: (0, 0)),
                  pl.BlockSpec(memory_space=pl.ANY)],
        out_specs=pl.BlockSpec((ts, d), lambda t, *_: (t, 0)),
        scratch_shapes=[pltpu.VMEM((2, tc, d), BF16), pltpu.SemaphoreType.DMA((2,))],
    )
    return pl.pallas_call(
        _dispatch_kernel,
        out_shape=jax.ShapeDtypeStruct((n_slots, d), BF16),
        grid_spec=grid_spec,
        compiler_params=_params("arbitrary"),
        name="moe_dispatch",
    )(*tables, pos1, pos2, h)


def _expert_ffn_kernel(texp_ref, nused_ref, xs_ref, wg_ref, wu_ref, wd_ref, o_ref, acc_ref):
    i = pl.program_id(0)
    f = pl.program_id(1)

    @pl.when(f == 0)
    def _():
        acc_ref[...] = jnp.zeros_like(acc_ref)

    @pl.when(i < nused_ref[0])
    def _():
        xs = xs_ref[...]
        gate = jnp.dot(xs, wg_ref[0], preferred_element_type=F32)
        up = jnp.dot(xs, wu_ref[0], preferred_element_type=F32)
        act = (_silu(gate) * up).astype(BF16)
        acc_ref[...] += jnp.dot(act, wd_ref[0], preferred_element_type=F32)

    @pl.when(f == pl.num_programs(1) - 1)
    def _():
        o_ref[...] = acc_ref[...].astype(o_ref.dtype)


def _expert_ffn(xs, tile_expert, n_used, wg, wu, wd, ts, tf):
    n_slots, d = xs.shape
    ff = wg.shape[2]

    def fchunk(i, f, nu):
        return jnp.where(i < nu[0], f, 0)

    grid_spec = pltpu.PrefetchScalarGridSpec(
        num_scalar_prefetch=2,
        grid=(n_slots // ts, ff // tf),
        in_specs=[pl.BlockSpec((ts, d), lambda i, f, te, nu: (i, 0)),
                  pl.BlockSpec((1, d, tf), lambda i, f, te, nu: (te[i], 0, fchunk(i, f, nu))),
                  pl.BlockSpec((1, d, tf), lambda i, f, te, nu: (te[i], 0, fchunk(i, f, nu))),
                  pl.BlockSpec((1, tf, d), lambda i, f, te, nu: (te[i], fchunk(i, f, nu), 0))],
        out_specs=pl.BlockSpec((ts, d), lambda i, f, te, nu: (i, 0)),
        scratch_shapes=[pltpu.VMEM((ts, d), F32)],
    )
    return pl.pallas_call(
        _expert_ffn_kernel,
        out_shape=jax.ShapeDtypeStruct((n_slots, d), BF16),
        grid_spec=grid_spec,
        compiler_params=_params("arbitrary", "arbitrary"),
        name="moe_expert_ffn",
    )(tile_expert, n_used, xs, wg, wu, wd)


def _combine_kernel(seg_row_ref, seg_blocks_ref, seg_first_ref, seg_end_ref,
                    route_ref, gstart_ref, ys_hbm, x_ref, gf_ref, o_ref,
                    acc_ref, cols_ref, ybuf_ref, sem_ref):
    c = pl.program_id(0)
    n_chunks = pl.num_programs(0)
    tc = x_ref.shape[0]

    def block_copy(chunk, buf, e, k, blk):
        row0 = pl.multiple_of(seg_row_ref[chunk * N_EXPERTS + e] + k * COMBINE_ROWS, ROW_ALIGN)
        return pltpu.make_async_copy(ys_hbm.at[pl.ds(row0, COMBINE_ROWS), :], ybuf_ref.at[buf, blk],
                                     sem_ref.at[buf])

    def for_each_block(chunk, fn):
        blk = 0
        for e in range(N_EXPERTS):
            n_blk = seg_blocks_ref[chunk * N_EXPERTS + e]
            for k in range(COMBINE_MAX_BLOCKS):
                pl.when(k < n_blk)(functools.partial(fn, e, k, blk + k))
            blk = blk + n_blk

    def start_reads(chunk, buf):
        for_each_block(chunk, lambda e, k, blk: block_copy(chunk, buf, e, k, blk).start())

    @pl.when(c == 0)
    def _():
        start_reads(0, 0)

    @pl.when(c + 1 < n_chunks)
    def _():
        start_reads(c + 1, (c + 1) % 2)

    buf = c % 2
    acc_ref[...] = x_ref[...]
    route = route_ref[...]
    lane = lax.broadcasted_iota(I32, (1, LANES), 1).astype(F32)
    for which in range(TOP_K):
        start = jnp.sum(jnp.where(lane == route[:, which:which + 1], gstart_ref[...], 0.0),
                        axis=1, keepdims=True)
        pos = start + route[:, 4 + which:5 + which]
        cols_ref[which] = jnp.broadcast_to(pos, (tc, LANES))
        cols_ref[TOP_K + which] = jnp.broadcast_to(route[:, 2 + which:3 + which], (tc, LANES))

    rep = COMBINE_ROWS // LANES
    wide = lambda idx: jnp.concatenate([cols_ref[idx]] * rep, axis=1)

    for_each_block(c, lambda e, k, blk: block_copy(c, buf, e, k, blk).wait())

    def gather_block(e, k, blk):
        row0 = seg_row_ref[c * N_EXPERTS + e] + k * COMBINE_ROWS
        slot = row0 + lax.broadcasted_iota(I32, (1, COMBINE_ROWS), 1)
        inside = (slot >= seg_first_ref[c * N_EXPERTS + e]) & (slot < seg_end_ref[c * N_EXPERTS + e])
        slot = jnp.where(inside, slot, -1).astype(F32)
        gmat = jnp.where(slot == wide(0), wide(2), jnp.where(slot == wide(1), wide(3), 0.0))
        acc_ref[...] += jnp.dot(gmat.astype(BF16), ybuf_ref[buf, blk], preferred_element_type=F32)

    for_each_block(c, gather_block)
    o_ref[...] = _rms(acc_ref[...], gf_ref[...])


def _combine(x2, ys, route, gstart_row, g_final, segs, tc):
    n, d = x2.shape
    max_blocks = (TOP_K * tc + N_EXPERTS * (ROW_ALIGN - 1)) // COMBINE_ROWS + N_EXPERTS
    grid_spec = pltpu.PrefetchScalarGridSpec(
        num_scalar_prefetch=len(segs),
        grid=(n // tc,),
        in_specs=[pl.BlockSpec((tc, LANES), lambda c, *_: (c, 0)),
                  pl.BlockSpec((1, LANES), lambda c, *_: (0, 0)),
                  pl.BlockSpec(memory_space=pl.ANY),
                  pl.BlockSpec((tc, d), lambda c, *_: (c, 0)),
                  pl.BlockSpec((1, d), lambda c, *_: (0, 0))],
        out_specs=pl.BlockSpec((tc, d), lambda c, *_: (c, 0)),
        scratch_shapes=[pltpu.VMEM((tc, d), F32), pltpu.VMEM((2 * TOP_K, tc, LANES), F32),
                        pltpu.VMEM((2, max_blocks, COMBINE_ROWS, d), BF16),
                        pltpu.SemaphoreType.DMA((2,))],
    )
    return pl.pallas_call(
        _combine_kernel,
        out_shape=jax.ShapeDtypeStruct((n, d), F32),
        grid_spec=grid_spec,
        compiler_params=_params("arbitrary"),
        name="moe_combine",
    )(*segs, route, gstart_row, ys, x2, g_final)


def _count_le(ends, v):
    return jnp.sum((ends[None, :] <= v[:, None]).astype(I32), axis=1)


def _moe_top2(x2, g, w_router, wg, wu, wd, g_final):
    n, d = x2.shape
    ts, tc = T_SLOT, min(T_CHUNK, n)
    n_chunks = n // tc
    n_tiles = TOP_K * n // ts + N_EXPERTS + 1
    n_slots = n_tiles * ts

    h, route, cstart, ctotal = _router(x2, g, w_router, tc)

    experts = jnp.arange(N_EXPERTS, dtype=I32)
    counts = ctotal[0, :N_EXPERTS].astype(I32)
    padded = (counts + ts - 1) // ts * ts
    group_end = jnp.cumsum(padded)
    group_start = group_end - padded
    n_used = (group_end[-1] // ts).astype(I32)
    tiles = jnp.arange(n_tiles, dtype=I32)
    tile_expert = jnp.minimum(_count_le(group_end // ts, tiles), N_EXPERTS - 1)
    c_before = cstart[:, 0, :N_EXPERTS].astype(I32)
    c_after = jnp.concatenate([c_before[1:], counts[None, :]], axis=0)

    e1, e2 = route[:, 0].astype(I32), route[:, 1].astype(I32)
    start_of = lambda e: jnp.sum(jnp.where(e[:, None] == experts, group_start, 0), axis=1)
    pos1 = start_of(e1) + route[:, 4].astype(I32)
    pos2 = start_of(e2) + route[:, 5].astype(I32)

    rank_lo = tiles * ts - group_start[tile_expert]
    rank_hi = jnp.minimum(rank_lo + ts, counts[tile_expert]) - 1
    after_t = c_after[:, tile_expert]
    used = tiles < n_used
    c_lo = jnp.where(used, jnp.sum((after_t <= rank_lo).astype(I32), axis=0), 0)
    c_hi = jnp.sum((after_t <= rank_hi).astype(I32), axis=0)
    d_tables = (tile_expert, c_lo, jnp.where(used, c_hi - c_lo + 1, 0), group_start,
                c_before.reshape(-1), c_after.reshape(-1))

    seg_first = group_start + c_before
    seg_row = seg_first // ROW_ALIGN * ROW_ALIGN
    seg_blocks = jnp.where(c_after > c_before,
                           (seg_first - seg_row + c_after - c_before + COMBINE_ROWS - 1) // COMBINE_ROWS, 0)

    xs = _dispatch(h, pos1.reshape(1, n), pos2.reshape(1, n), d_tables, n_slots, ts, tc)
    ys = _expert_ffn(xs, tile_expert, n_used.reshape(1), wg, wu, wd, ts, min(TF_EXPERT, wg.shape[2]))
    gstart_row = _pad_cols(group_start.astype(F32).reshape(1, N_EXPERTS), LANES)
    segs = tuple(t.reshape(-1) for t in (seg_row, seg_blocks, seg_first, seg_first + c_after - c_before))
    return _combine(x2, ys, route, gstart_row, g_final, segs, tc)


def _pad_cols(w, width):
    return jnp.pad(w, ((0, 0), (0, width - w.shape[1])))


def _prep_w_in(w):
    parts = []
    off = 0
    for size in IN_SIZES:
        parts.append(w[:, off:off + size])
        off += size
    cq, ckv, kr, q_b, k_b, v_b, q_c, k_c, v_c, f_c, q_d, k_d, v_d = parts
    d = w.shape[0]
    kr_blk = jnp.concatenate([jnp.zeros((d, MLA_NOPE), w.dtype), kr,
                              jnp.zeros((d, LANES - MLA_NOPE - MLA_ROPE), w.dtype)], axis=1)
    cols = [cq, ckv, kr_blk, _pad_cols(f_c, LANES), q_b, k_b, v_b, q_c, k_c, v_c, q_d, k_d, v_d]
    return jnp.concatenate(cols, axis=1).astype(BF16)


def _prep_mla(w_uq, w_ukv):
    qk = MLA_NOPE + MLA_ROPE
    wuq = jnp.pad(w_uq.reshape(MLA_Q_RANK, GROUP_HEADS, qk), ((0, 0), (0, 0), (0, LANES - qk)))
    wuq = wuq.reshape(MLA_Q_RANK, GROUP_HEADS * LANES)
    kv = w_ukv.reshape(MLA_KV_RANK, GROUP_HEADS, MLA_NOPE + HEAD_DIM)
    wuk = jnp.pad(kv[:, :, :MLA_NOPE], ((0, 0), (0, 0), (0, LANES - MLA_NOPE)))
    wuk = wuk.reshape(MLA_KV_RANK, GROUP_HEADS * LANES)
    wuv = kv[:, :, MLA_NOPE:].reshape(MLA_KV_RANK, GROUP_WIDTH)
    return wuq.astype(BF16), wuk.astype(BF16), wuv.astype(BF16)


def _row(v, width=None):
    v = v.reshape(1, -1).astype(F32)
    return v if width is None else _pad_cols(v, width)


def kernel(x, g_mix, w_in, b_forget, g_q_lat, g_kv_lat, w_uq, w_ukv, lambda_q1, lambda_k1, lambda_q2,
           lambda_k2, g_diff_sub, w_out, g_ffn, w_ffn_gate, w_ffn_up, w_ffn_down, w_router, w_exp_gate,
           w_exp_up, w_exp_down, g_final):
    b, s_len, d = x.shape
    depth = g_mix.shape[0]
    assert d == D_MODEL and depth == 2, "kernel is specialised to the two-layer trunk"
    assert w_router.shape[-1] == N_EXPERTS
    n = b * s_len
    t_attn = T_ATTN
    tm_in = min(TM_PROJ, s_len)
    tm_tok = min(TM_TOK, n)
    assert s_len % t_attn == 0 and s_len % tm_in == 0 and n % tm_tok == 0
    assert n % min(T_CHUNK, n) == 0 and (TOP_K * n) % T_SLOT == 0

    tabs = _rope_tables(s_len)
    log_mult = _dilated_log2_multiplicity(t_attn)
    n_lm = log_mult.shape[0]

    for l in range(depth):
        wuq, wuk, wuv = _prep_mla(w_uq[l], w_ukv[l])
        zz, vt, zf = _proj_in(x, _row(g_mix[l]), _prep_w_in(w_in[l]), _row(g_q_lat[l]),
                              _row(g_kv_lat[l]), wuq, wuk, wuv, tabs, tm_in)
        crep, crow = _forget_cumsum(zf, _row(b_forget[l], LANES), tm_in)

        lam_init = 0.8 - 0.6 * math.exp(-0.3 * l)
        lamv = jnp.concatenate(
            [_row(v, LANES) for v in (lambda_q1[l], lambda_k1[l], lambda_q2[l], lambda_k2[l])]
            + [jnp.zeros((4, LANES), F32)], axis=0)
        gsub = jnp.tile(_row(g_diff_sub[l]), (1, GROUP_HEADS))

        y_a = _attention(zz, vt, 0, "mla", t_attn, ZZ_QA, ZZ_KA)
        c0 = ZZ_QB
        y_b = _attention(zz, vt, 1, "dil", t_attn, c0, c0 + GROUP_WIDTH, extra=(log_mult,),
                         extra_specs=(pl.BlockSpec((n_lm, t_attn, t_attn), lambda bi, qi: (0, 0, 0)),))
        c0 += 2 * GROUP_WIDTH
        y_c = _attention(zz, vt, 2, "fox", t_attn, c0, c0 + GROUP_WIDTH, extra=(crep, crow),
                         extra_specs=(pl.BlockSpec((1, GROUP_HEADS, s_len, LANES), lambda bi, qi: (bi, 0, 0, 0)),
                                      pl.BlockSpec((1, 8, s_len), lambda bi, qi: (bi, 0, 0))))
        c0 += 2 * GROUP_WIDTH
        y_d = _attention(zz, vt, 3, "diff", t_attn, c0, c0 + GROUP_WIDTH, extra=(lamv, gsub),
                         extra_specs=(pl.BlockSpec((8, LANES), lambda bi, qi: (0, 0)),
                                      pl.BlockSpec((1, GROUP_WIDTH), lambda bi, qi: (0, 0))),
                         lam_init=lam_init)

        ys = [y.reshape(n, GROUP_WIDTH) for y in (y_a, y_b, y_c, y_d)]
        x2 = _proj_out(x.reshape(n, d), ys, w_out[l].astype(BF16), tm_tok)

        i = l // 2
        if l % 2 == 0:
            x2 = _ffn(x2, _row(g_ffn[l]), w_ffn_gate[i].astype(BF16), w_ffn_up[i].astype(BF16),
                      w_ffn_down[i].astype(BF16), tm_tok, TF_DENSE)
        else:
            x2 = _moe_top2(x2, _row(g_ffn[l]), _pad_cols(w_router[i].astype(F32), LANES),
                           w_exp_gate[i].astype(BF16), w_exp_up[i].astype(BF16),
                           w_exp_down[i].astype(BF16), _row(g_final))
        x = x2.reshape(b, s_len, d)
    return x
```
